```python
import math
import functools
import jax
import jax.numpy as jnp
from jax import lax
import numpy as np

D_MODEL = 2048
BATCH = 1
SEQ = 8192
DEPTH = 1
DEC_BATCH = 32
DEC_SEQ = 1
PAST_LEN = 16384
PAGE_SIZE = 128

A_HEAD_DIM = 128
A_HEADS = D_MODEL // A_HEAD_DIM
A_WIDTH = A_HEADS * A_HEAD_DIM
MOBA_BLOCK = 256
MOBA_TOPK = 3
ROPE_THETA = 10000.0
B_KEY_DIM = 128
B_VAL_DIM = 128
B_HEADS = D_MODEL // B_VAL_DIM
B_KEY_WIDTH = B_HEADS * B_KEY_DIM
B_VAL_WIDTH = B_HEADS * B_VAL_DIM
CONV_WIDTH = 4
CONV_DIM = 2 * B_KEY_WIDTH + B_VAL_WIDTH
DELTA_CHUNK = 64
D_FF = 5632
PLE_DIM = 256
IN_SPLITS = (A_WIDTH, A_WIDTH, A_WIDTH, CONV_DIM, B_VAL_WIDTH, B_HEADS, B_HEADS, D_MODEL, D_MODEL)
D_IN = 3 * A_WIDTH + CONV_DIM + B_VAL_WIDTH + 2 * B_HEADS + 2 * D_MODEL
RMS_EPS = 1e-6
L2_EPS = 1e-6
NEG_INF = -1e30
QUERY_ROWS_PER_STEP = 32
POOL_NUM = 5
POOL_DEN = 4

kernel_name = "hybrid_moba_gdn_macaron_step"


def split_points():
    pts, acc = [], 0
    for s in IN_SPLITS[:-1]:
        acc += s
        pts.append(acc)
    return pts


def rms_norm(x, gain):
    xf = x.astype(jnp.float32)
    y = xf * lax.rsqrt(jnp.mean(xf * xf, axis=-1, keepdims=True) + RMS_EPS)
    return (y * gain.astype(jnp.float32)).astype(x.dtype)


def l2_normalize(x):
    return x * lax.rsqrt(jnp.sum(x * x, axis=-1, keepdims=True) + L2_EPS)


def rotary(x, pos):
    half = x.shape[-1] // 2
    inv_freq = ROPE_THETA ** (-jnp.arange(half, dtype=jnp.float32) / half)
    ang = pos.astype(jnp.float32)[:, None] * inv_freq[None, :]
    cos = jnp.cos(ang)[None, :, None, :]
    sin = jnp.sin(ang)[None, :, None, :]
    xf = x.astype(jnp.float32)
    x1, x2 = xf[..., :half], xf[..., half:]
    return jnp.concatenate([x1 * cos - x2 * sin, x2 * cos + x1 * sin], axis=-1).astype(x.dtype)


def swiglu(x, w_up, w_down):
    gate, up = jnp.split(x @ w_up, 2, axis=-1)
    return (jax.nn.silu(gate) * up) @ w_down


def moba_attend(q, q_pos, k_means, gather_blocks):
    b, t, h, dh = q.shape
    nb_real = k_means.shape[1]
    if nb_real < MOBA_TOPK:
        k_means = jnp.pad(k_means, ((0, 0), (0, MOBA_TOPK - nb_real), (0, 0), (0, 0)))
    nb = k_means.shape[1]
    c = math.gcd(t, max(1, QUERY_ROWS_PER_STEP // b))
    n = t // c
    scale = dh ** -0.5
    qs = q.reshape(b, n, c, h, dh).transpose(1, 0, 2, 3, 4)
    ps = q_pos.reshape(n, c)
    blk_ids = jnp.arange(nb, dtype=jnp.int32)
    offs = jnp.arange(MOBA_BLOCK, dtype=jnp.int32)

    def step(args):
        qc, pc = args
        own = pc // MOBA_BLOCK
        gate = jnp.einsum('bqhd,bnhd->bqhn', qc, k_means, preferred_element_type=jnp.float32)
        fully_past = blk_ids[None, :] < own[:, None]
        gate = jnp.where(fully_past[None, :, None, :], gate, NEG_INF)
        _, sel = lax.top_k(gate, MOBA_TOPK)
        sel_valid = sel < own[None, :, None, None]
        own_b = jnp.broadcast_to(own[None, :, None, None], (b, c, h, 1)).astype(jnp.int32)
        idx = jnp.concatenate([jnp.minimum(sel, nb_real - 1).astype(jnp.int32), own_b], axis=-1)
        valid = jnp.concatenate([sel_valid, jnp.ones_like(sel_valid[..., :1])], axis=-1)
        kg, vg = gather_blocks(idx)
        logits = jnp.einsum('bqhd,bqhskd->bqhsk', qc, kg, preferred_element_type=jnp.float32) * scale
        key_pos = idx[..., None] * MOBA_BLOCK + offs
        mask = valid[..., None] & (key_pos <= pc[None, :, None, None, None])
        logits = jnp.where(mask, logits, NEG_INF)
        probs = jax.nn.softmax(logits.reshape(b, c, h, -1), axis=-1).reshape(logits.shape)
        return jnp.einsum('bqhsk,bqhskd->bqhd', probs.astype(vg.dtype), vg)

    out = lax.map(step, (qs, ps))
    return out.transpose(1, 0, 2, 3, 4).reshape(b, t, h, dh)


def prompt_blocks(k, v):
    b, s, h, dh = k.shape
    nb = -(-s // MOBA_BLOCK)
    pad = ((0, 0), (0, nb * MOBA_BLOCK - s), (0, 0), (0, 0))
    kblk = jnp.pad(k, pad).reshape(b, nb, MOBA_BLOCK, h, dh)
    vblk = jnp.pad(v, pad).reshape(b, nb, MOBA_BLOCK, h, dh)
    k_means = jnp.mean(kblk, axis=2, dtype=jnp.float32)
    kt = kblk.transpose(0, 3, 1, 2, 4)
    vt = vblk.transpose(0, 3, 1, 2, 4)
    bi = jnp.arange(b)[:, None, None, None]
    hi = jnp.arange(h)[None, None, :, None]

    def gather(idx):
        return kt[bi, hi, idx], vt[bi, hi, idx]

    return k_means, gather


def paged_blocks(cache_k, cache_v, page_means, layer, page_table, k_new, v_new):
    b, t, h, dh = k_new.shape
    n_past_pages = page_table.shape[1]
    ppb = MOBA_BLOCK // PAGE_SIZE
    n_tail = -(-t // PAGE_SIZE)
    pad = ((0, 0), (0, n_tail * PAGE_SIZE - t), (0, 0), (0, 0))
    kt = jnp.pad(k_new, pad).reshape(b, n_tail, PAGE_SIZE, h, dh)
    vt = jnp.pad(v_new, pad).reshape(b, n_tail, PAGE_SIZE, h, dh)
    past_means = page_means[page_table]
    tail_means = jnp.sum(kt, axis=2, dtype=jnp.float32) / PAGE_SIZE
    pm = jnp.concatenate([past_means, tail_means], axis=1)
    n_pages = n_past_pages + n_tail
    nb = -(-n_pages // ppb)
    pm = jnp.pad(pm, ((0, 0), (0, nb * ppb - n_pages), (0, 0), (0, 0)))
    k_means = pm.reshape(b, nb, ppb, h, dh).mean(axis=2)
    bi = jnp.arange(b)[:, None, None, None, None]
    hi = jnp.arange(h)[None, None, :, None, None]
    page_ids = jnp.arange(ppb, dtype=jnp.int32)

    def gather(idx):
        lp = idx[..., None] * ppb + page_ids
        in_cache = (lp < n_past_pages)[..., None, None]
        phys = page_table[bi, jnp.minimum(lp, n_past_pages - 1)]
        tail = jnp.clip(lp - n_past_pages, 0, n_tail - 1)
        kg = jnp.where(in_cache, cache_k[layer, phys, :, hi], kt[bi, tail, :, hi])
        vg = jnp.where(in_cache, cache_v[layer, phys, :, hi], vt[bi, tail, :, hi])
        shape = idx.shape + (MOBA_BLOCK, dh)
        return kg.reshape(shape), vg.reshape(shape)

    return k_means, gather


def attend_prompt(q, k, v, pos):
    k_means, gather = prompt_blocks(k, v)
    return moba_attend(q, pos, k_means, gather)


def attend_paged(q, k, v, pos, cache_k, cache_v, page_means, layer, page_table):
    k_means, gather = paged_blocks(cache_k, cache_v, page_means, layer, page_table, k, v)
    return moba_attend(q, pos, k_means, gather)


def gated_delta_chunked(q, k, v, g, beta, s0):
    bsz, t, h, dk = q.shape
    dv = v.shape[-1]
    c = math.gcd(t, DELTA_CHUNK)
    n = t // c

    def blocks(x):
        return x.reshape((bsz, n, c, h) + x.shape[3:]).swapaxes(0, 1).swapaxes(2, 3)

    qc, kc, vc, gc, bc = blocks(q), blocks(k), blocks(v), blocks(g), blocks(beta)
    gcum = jnp.cumsum(gc, axis=-1)
    tril = jnp.tril(jnp.ones((c, c), dtype=bool))
    strict = jnp.tril(jnp.ones((c, c), dtype=bool), k=-1)
    decay = jnp.where(tril, jnp.exp(jnp.minimum(gcum[..., :, None] - gcum[..., None, :], 0.0)), 0.0)
    kb = kc * bc[..., None]
    lower = jnp.where(strict, jnp.einsum('nbhid,nbhjd->nbhij', kb, kc) * decay, 0.0)
    tmat = lower + jnp.eye(c, dtype=jnp.float32)
    u = lax.linalg.triangular_solve(tmat, vc * bc[..., None], left_side=True, lower=True, unit_diagonal=True)
    w = lax.linalg.triangular_solve(tmat, kb * jnp.exp(gcum)[..., None], left_side=True, lower=True, unit_diagonal=True)
    qk = jnp.einsum('nbhid,nbhjd->nbhij', qc, kc) * decay

    def step(s, xs):
        q_i, k_i, u_i, w_i, qk_i, g_i = xs
        v_new = u_i - jnp.einsum('bhck,bhkv->bhcv', w_i, s)
        o_i = jnp.einsum('bhck,bhkv->bhcv', q_i * jnp.exp(g_i)[..., None], s) + jnp.einsum('bhij,bhjv->bhiv', qk_i, v_new)
        g_last = g_i[..., -1:]
        s = s * jnp.exp(g_last)[..., None] + jnp.einsum('bhck,bhcv->bhkv', k_i * jnp.exp(g_last - g_i)[..., None], v_new)
        return s, o_i

    s_final, o = lax.scan(step, s0, (qc, kc, u, w, qk, gcum))
    o = o.swapaxes(2, 3).swapaxes(0, 1).reshape(bsz, t, h, dv)
    return o, s_final


def deltanet_branch(qkv_raw, z, a, b_logit, conv_prev, delta_prev, conv_w, a_log, dt_bias, o_gain):
    bsz, t, _ = qkv_raw.shape
    xp = jnp.concatenate([conv_prev.astype(qkv_raw.dtype), qkv_raw], axis=1)
    conv = sum(xp[:, j:j + t] * conv_w[j] for j in range(CONV_WIDTH))
    new_conv = xp[:, t:]
    qkv = jax.nn.silu(conv).astype(jnp.float32)
    q = qkv[..., :B_KEY_WIDTH].reshape(bsz, t, B_HEADS, B_KEY_DIM)
    k = qkv[..., B_KEY_WIDTH:2 * B_KEY_WIDTH].reshape(bsz, t, B_HEADS, B_KEY_DIM)
    v = qkv[..., 2 * B_KEY_WIDTH:].reshape(bsz, t, B_HEADS, B_VAL_DIM)
    q = l2_normalize(q) * (B_KEY_DIM ** -0.5)
    k = l2_normalize(k)
    beta = jax.nn.sigmoid(b_logit.astype(jnp.float32))
    g = -jnp.exp(a_log.astype(jnp.float32)) * jax.nn.softplus(a.astype(jnp.float32) + dt_bias.astype(jnp.float32))
    o, new_delta = gated_delta_chunked(q, k, v, g, beta, delta_prev.astype(jnp.float32))
    o = rms_norm(o, o_gain) * jax.nn.silu(z.astype(jnp.float32).reshape(bsz, t, B_HEADS, B_VAL_DIM))
    return o.reshape(bsz, t, B_VAL_WIDTH).astype(qkv_raw.dtype), new_delta, new_conv


def layer_forward(x, pe, pos, attend, conv_prev, delta_prev, lw):
    (ffn1_g, ffn1_up, ffn1_down, mix_g, w_in, qn_g, kn_g, conv_w, a_log, dt_bias, o_g, w_out,
     ffn2_g, ffn2_up, ffn2_down, ple_g, ple_gate, ple_proj) = lw
    bsz, t, _ = x.shape
    x = x + 0.5 * swiglu(rms_norm(x, ffn1_g), ffn1_up, ffn1_down)
    h = rms_norm(x, mix_g)
    qa, ka, va, qkv_b, z_b, a_b, b_b, gate_a, gate_b = jnp.split(h @ w_in, split_points(), axis=-1)
    heads = (bsz, t, A_HEADS, A_HEAD_DIM)
    qa = rotary(rms_norm(qa.reshape(heads), qn_g), pos)
    ka = rotary(rms_norm(ka.reshape(heads), kn_g), pos)
    va = va.reshape(heads)
    o_a = attend(qa, ka, va, pos).reshape(bsz, t, A_WIDTH)
    o_b, delta_new, conv_new = deltanet_branch(qkv_b, z_b, a_b, b_b, conv_prev, delta_prev,
                                               conv_w, a_log, dt_bias, o_g)
    merged = jax.nn.sigmoid(gate_a) * o_a + jax.nn.sigmoid(gate_b) * o_b
    x = x + merged @ w_out
    x = x + 0.5 * swiglu(rms_norm(x, ffn2_g), ffn2_up, ffn2_down)
    x = x + jax.nn.sigmoid(rms_norm(x, ple_g) @ ple_gate) * (pe @ ple_proj)
    return x, ka, va, delta_new, conv_new


def setup_inputs(seed: int = 0) -> dict:
    key = jax.random.key(seed)
    ks = iter(jax.random.split(key, 40))
    f32 = jnp.float32
    n_pages = PAST_LEN // PAGE_SIZE
    n_pool = (DEC_BATCH * n_pages * POOL_NUM) // POOL_DEN

    def normal(shape, scale):
        return jax.random.normal(next(ks), shape, f32) * scale

    def gain(n):
        return 1.0 + normal((DEPTH, n), 0.02)

    page_table = jax.random.permutation(next(ks), n_pool)[: DEC_BATCH * n_pages].reshape(DEC_BATCH, n_pages).astype(jnp.int32)
    dt = jnp.exp(jax.random.uniform(next(ks), (DEPTH, B_HEADS), f32, math.log(1e-3), math.log(1e-1)))
    a_log = jnp.log(jax.random.uniform(next(ks), (DEPTH, B_HEADS), f32, 1.0, 16.0))
    return {
        "x_prompt": normal((BATCH, SEQ, D_MODEL), 1.0),
        "x_sample": normal((DEC_BATCH, DEC_SEQ, D_MODEL), 1.0),
        "cache_k": normal((DEPTH, n_pool, PAGE_SIZE, A_HEADS, A_HEAD_DIM), 1.0),
        "cache_v": normal((DEPTH, n_pool, PAGE_SIZE, A_HEADS, A_HEAD_DIM), 1.0),
        "state_conv": normal((DEPTH, DEC_BATCH, CONV_WIDTH - 1, CONV_DIM), 1.0),
        "state_delta": normal((DEPTH, DEC_BATCH, B_HEADS, B_KEY_DIM, B_VAL_DIM), 0.1),
        "page_table": page_table,
        "p_prompt": normal((DEPTH, BATCH, SEQ, PLE_DIM), 1.0),
        "p_sample": normal((DEPTH, DEC_BATCH, DEC_SEQ, PLE_DIM), 1.0),
        "ffn1_norm": gain(D_MODEL),
        "ffn1_w_up": normal((DEPTH, D_MODEL, 2 * D_FF), D_MODEL ** -0.5),
        "ffn1_w_down": normal((DEPTH, D_FF, D_MODEL), D_FF ** -0.5),
        "mix_norm": gain(D_MODEL),
        "w_in": normal((DEPTH, D_MODEL, D_IN), D_MODEL ** -0.5),
        "q_norm": gain(A_HEAD_DIM),
        "k_norm": gain(A_HEAD_DIM),
        "conv_w": normal((DEPTH, CONV_WIDTH, CONV_DIM), CONV_WIDTH ** -0.5),
        "a_log": a_log,
        "dt_bias": dt + jnp.log(-jnp.expm1(-dt)),
        "o_norm": gain(B_VAL_DIM),
        "w_out": normal((DEPTH, D_MODEL, D_MODEL), D_MODEL ** -0.5),
        "ffn2_norm": gain(D_MODEL),
        "ffn2_w_up": normal((DEPTH, D_MODEL, 2 * D_FF), D_MODEL ** -0.5),
        "ffn2_w_down": normal((DEPTH, D_FF, D_MODEL), D_FF ** -0.5),
        "ple_norm": gain(D_MODEL),
        "ple_gate": normal((DEPTH, D_MODEL, D_MODEL), D_MODEL ** -0.5),
        "ple_proj": normal((DEPTH, PLE_DIM, D_MODEL), PLE_DIM ** -0.5),
    }


def reference(x_prompt, x_sample, cache_k, cache_v, state_conv, state_delta, page_table, p_prompt, p_sample,
              ffn1_norm, ffn1_w_up, ffn1_w_down, mix_norm, w_in, q_norm, k_norm, conv_w, a_log, dt_bias,
              o_norm, w_out, ffn2_norm, ffn2_w_up, ffn2_w_down, ple_norm, ple_gate, ple_proj):
    pos_prompt = jnp.arange(x_prompt.shape[1], dtype=jnp.int32)
    past_len = page_table.shape[1] * PAGE_SIZE
    pos_sample = past_len + jnp.arange(x_sample.shape[1], dtype=jnp.int32)
    page_means_k = jnp.mean(cache_k, axis=2, dtype=jnp.float32)
    yp, ys = x_prompt, x_sample
    kp, vp, cp, dp, ksm, vsm, csm, dsm = [], [], [], [], [], [], [], []
    for i in range(DEPTH):
        lw = (ffn1_norm[i], ffn1_w_up[i], ffn1_w_down[i], mix_norm[i], w_in[i], q_norm[i], k_norm[i],
              conv_w[i], a_log[i], dt_bias[i], o_norm[i], w_out[i], ffn2_norm[i], ffn2_w_up[i],
              ffn2_w_down[i], ple_norm[i], ple_gate[i], ple_proj[i])
        conv0 = jnp.zeros((x_prompt.shape[0], CONV_WIDTH - 1, CONV_DIM), x_prompt.dtype)
        delta0 = jnp.zeros((x_prompt.shape[0], B_HEADS, B_KEY_DIM, B_VAL_DIM), jnp.float32)
        yp, k_p, v_p, d_p, c_p = layer_forward(yp, p_prompt[i], pos_prompt, attend_prompt, conv0, delta0, lw)
        attend_s = functools.partial(attend_paged, cache_k=cache_k, cache_v=cache_v,
                                     page_means=page_means_k[i], layer=i, page_table=page_table)
        ys, k_s, v_s, d_s, c_s = layer_forward(ys, p_sample[i], pos_sample, attend_s, state_conv[i], state_delta[i], lw)
        kp.append(k_p); vp.append(v_p); cp.append(c_p); dp.append(d_p)
        ksm.append(k_s); vsm.append(v_s); csm.append(c_s); dsm.append(d_s)
    return (yp, ys, jnp.stack(kp), jnp.stack(vp), jnp.stack(cp), jnp.stack(dp),
            jnp.stack(ksm), jnp.stack(vsm), jnp.stack(csm), jnp.stack(dsm))
```

```python
import functools
import math

import jax
import jax.numpy as jnp
from jax import lax
from jax.experimental import pallas as pl
from jax.experimental.pallas import tpu as pltpu

F32 = jnp.float32
BF16 = jnp.bfloat16
HI = lax.Precision.HIGHEST

HEAD_DIM = 128
MOBA_BLOCK = 256
MOBA_TOPK = 3
PAGE_SIZE = 128
CONV_WIDTH = 4
DELTA_CHUNK = 64
ROPE_THETA = 10000.0
RMS_EPS = 1e-6
L2_EPS = 1e-6
NEG_INF = -1e30

LANE = 128
SUBLANE = 8
VMEM_LIMIT_BYTES = 56 * 1024 * 1024


def _params(*sem):
    return pltpu.CompilerParams(dimension_semantics=sem, vmem_limit_bytes=VMEM_LIMIT_BYTES)


def _tile(n, pref, quantum):
    if n <= pref:
        return n
    t = (pref // quantum) * quantum
    while t > quantum and n % t:
        t -= quantum
    assert n % t == 0, (n, pref, quantum)
    return t


def _rms(x, gain):
    return x * lax.rsqrt(jnp.mean(x * x, axis=-1, keepdims=True) + RMS_EPS) * gain


def _dot(a, b, precision=None):
    return jnp.dot(a, b, preferred_element_type=F32, precision=precision)


def _dot_nt(a, b, precision=None):
    return lax.dot_general(a, b, (((1,), (1,)), ((), ())), preferred_element_type=F32, precision=precision)


def _dot_tn(a, b, precision=None):
    return lax.dot_general(a, b, (((0,), (0,)), ((), ())), preferred_element_type=F32, precision=precision)


def _rmsnorm_kernel(x_ref, g_ref, o_ref):
    o_ref[...] = _rms(x_ref[...], g_ref[...]).astype(o_ref.dtype)


def _rmsnorm(x, gain):
    m, d = x.shape
    tm = _tile(m, 512, SUBLANE)
    return pl.pallas_call(
        _rmsnorm_kernel,
        grid=(m // tm,),
        in_specs=[pl.BlockSpec((tm, d), lambda i: (i, 0)), pl.BlockSpec((1, d), lambda i: (0, 0))],
        out_specs=pl.BlockSpec((tm, d), lambda i: (i, 0)),
        out_shape=jax.ShapeDtypeStruct((m, d), BF16),
        compiler_params=_params("parallel"),
        name="rmsnorm",
    )(x, gain)


def _ffn_kernel(x_ref, h_ref, wg_ref, wu_ref, wd_ref, ng_ref, y_ref, hn_ref, acc_ref, *, nf):
    f = pl.program_id(1)

    @pl.when(f == 0)
    def _():
        acc_ref[...] = jnp.zeros_like(acc_ref)

    h = h_ref[...]
    g = _dot(h, wg_ref[...])
    u = _dot(h, wu_ref[...])
    a = (g * jax.nn.sigmoid(g) * u).astype(BF16)
    acc_ref[...] += _dot(a, wd_ref[...])

    @pl.when(f == nf - 1)
    def _():
        y = x_ref[...] + 0.5 * acc_ref[...]
        y_ref[...] = y
        hn_ref[...] = _rms(y, ng_ref[...]).astype(hn_ref.dtype)


def _ffn(x, h, w_up, w_down, next_gain):
    m, d = x.shape
    ff = w_down.shape[0]
    tm = _tile(m, 512, SUBLANE)
    tf = _tile(ff, 512, LANE)
    nf = ff // tf
    return pl.pallas_call(
        functools.partial(_ffn_kernel, nf=nf),
        grid=(m // tm, nf),
        in_specs=[
            pl.BlockSpec((tm, d), lambda i, f: (i, 0)),
            pl.BlockSpec((tm, d), lambda i, f: (i, 0)),
            pl.BlockSpec((d, tf), lambda i, f: (0, f)),
            pl.BlockSpec((d, tf), lambda i, f: (0, f + nf)),
            pl.BlockSpec((tf, d), lambda i, f: (f, 0)),
            pl.BlockSpec((1, d), lambda i, f: (0, 0)),
        ],
        out_specs=[pl.BlockSpec((tm, d), lambda i, f: (i, 0)), pl.BlockSpec((tm, d), lambda i, f: (i, 0))],
        out_shape=[jax.ShapeDtypeStruct((m, d), F32), jax.ShapeDtypeStruct((m, d), BF16)],
        scratch_shapes=[pltpu.VMEM((tm, d), F32)],
        compiler_params=_params("parallel", "arbitrary"),
        name="ffn",
    )(x, h, w_up, w_up, w_down, next_gain)


def _mm_kernel(h_ref, w_ref, o_ref):
    o_ref[...] = _dot(h_ref[...], w_ref[...])


def _matmul(h, w):
    m, kd = h.shape
    n = w.shape[1]
    tm = _tile(m, 512, SUBLANE)
    tn = _tile(n, 1024, LANE)
    return pl.pallas_call(
        _mm_kernel,
        grid=(m // tm, n // tn),
        in_specs=[pl.BlockSpec((tm, kd), lambda i, j: (i, 0)), pl.BlockSpec((kd, tn), lambda i, j: (0, j))],
        out_specs=pl.BlockSpec((tm, tn), lambda i, j: (i, j)),
        out_shape=jax.ShapeDtypeStruct((m, n), F32),
        compiler_params=_params("parallel", "arbitrary"),
        name="proj",
    )(h, w)


def _outproj_kernel(x_ref, oa_ref, ob_ref, ga_ref, gb_ref, w_ref, ng_ref, y_ref, hn_ref):
    merged = jax.nn.sigmoid(ga_ref[...]) * oa_ref[...] + jax.nn.sigmoid(gb_ref[...]) * ob_ref[...]
    y = x_ref[...] + _dot(merged.astype(BF16), w_ref[...])
    y_ref[...] = y
    hn_ref[...] = _rms(y, ng_ref[...]).astype(hn_ref.dtype)


def _outproj(x, o_a, o_b, gates, w_out, next_gain):
    m, d = x.shape
    tm = _tile(m, 256, SUBLANE)
    row = lambda i: (i, 0)
    return pl.pallas_call(
        _outproj_kernel,
        grid=(m // tm,),
        in_specs=[
            pl.BlockSpec((tm, d), row),
            pl.BlockSpec((tm, d), row),
            pl.BlockSpec((tm, d), row),
            pl.BlockSpec((tm, d), lambda i: (i, 0)),
            pl.BlockSpec((tm, d), lambda i: (i, 1)),
            pl.BlockSpec((d, d), lambda i: (0, 0)),
            pl.BlockSpec((1, d), lambda i: (0, 0)),
        ],
        out_specs=[pl.BlockSpec((tm, d), row), pl.BlockSpec((tm, d), row)],
        out_shape=[jax.ShapeDtypeStruct((m, d), F32), jax.ShapeDtypeStruct((m, d), BF16)],
        compiler_params=_params("parallel"),
        name="outproj",
    )(x, o_a, o_b, gates, gates, w_out, next_gain)


def _ple_kernel(x_ref, h_ref, pe_ref, wg_ref, wp_ref, y_ref):
    gate = jax.nn.sigmoid(_dot(h_ref[...], wg_ref[...]))
    y_ref[...] = x_ref[...] + gate * _dot(pe_ref[...], wp_ref[...])


def _ple(x, h, pe, w_gate, w_proj):
    m, d = x.shape
    pd = pe.shape[1]
    tm = _tile(m, 512, SUBLANE)
    tn = _tile(d, 1024, LANE)
    return pl.pallas_call(
        _ple_kernel,
        grid=(m // tm, d // tn),
        in_specs=[
            pl.BlockSpec((tm, tn), lambda i, j: (i, j)),
            pl.BlockSpec((tm, d), lambda i, j: (i, 0)),
            pl.BlockSpec((tm, pd), lambda i, j: (i, 0)),
            pl.BlockSpec((d, tn), lambda i, j: (0, j)),
            pl.BlockSpec((pd, tn), lambda i, j: (0, j)),
        ],
        out_specs=pl.BlockSpec((tm, tn), lambda i, j: (i, j)),
        out_shape=jax.ShapeDtypeStruct((m, d), F32),
        compiler_params=_params("parallel", "arbitrary"),
        name="ple",
    )(x, h, pe, w_gate, w_proj)


def _rope_kernel(xq_ref, xk_ref, gq_ref, gk_ref, c_ref, s_ref, q_ref, k_ref):
    c = c_ref[...]
    s = s_ref[...]

    def norm_rot(x, gain):
        y = _rms(x, gain)
        return y * c + pltpu.roll(y, HEAD_DIM // 2, 1) * s

    q_ref[...] = norm_rot(xq_ref[...], gq_ref[...])
    k_ref[...] = norm_rot(xk_ref[...], gk_ref[...])


def _rope(qk_raw, q_gain, k_gain, cos_tab, sin_tab):
    m, w2 = qk_raw.shape
    nh = w2 // (2 * HEAD_DIM)
    tm = _tile(m, 1024, SUBLANE)
    blk = lambda off: pl.BlockSpec((tm, HEAD_DIM), lambda i, h: (i, h + off))
    vec = pl.BlockSpec((1, HEAD_DIM), lambda i, h: (0, 0))
    tab = pl.BlockSpec((tm, HEAD_DIM), lambda i, h: (i, 0))
    out = jax.ShapeDtypeStruct((m, nh * HEAD_DIM), F32)
    return pl.pallas_call(
        _rope_kernel,
        grid=(m // tm, nh),
        in_specs=[blk(0), blk(nh), vec, vec, tab, tab],
        out_specs=[blk(0), blk(0)],
        out_shape=[out, out],
        compiler_params=_params("parallel", "arbitrary"),
        name="rope",
    )(qk_raw, qk_raw, q_gain, k_gain, cos_tab, sin_tab)


def _rope_tables(pos):
    half = HEAD_DIM // 2
    inv_freq = ROPE_THETA ** (-jnp.arange(half, dtype=F32) / half)
    ang = pos.astype(F32)[:, None] * inv_freq[None, :]
    cos, sin = jnp.cos(ang), jnp.sin(ang)
    return jnp.concatenate([cos, cos], axis=1), jnp.concatenate([-sin, sin], axis=1)


def _topk_mask(gate, valid, axis):
    n = gate.shape[axis]
    idx = lax.broadcasted_iota(jnp.int32, gate.shape, axis)
    g = jnp.where(valid, gate, NEG_INF)
    sel = jnp.zeros(gate.shape, F32)
    for _ in range(MOBA_TOPK):
        top = jnp.max(g, axis=axis, keepdims=True)
        first = jnp.min(jnp.where(g == top, idx, n), axis=axis, keepdims=True)
        hit = idx == first
        sel = jnp.where(hit, jnp.where(valid, 1.0, 0.0), sel)
        g = jnp.where(hit, -jnp.inf, g)
    return sel


def _moba_prompt_kernel(q_ref, k_ref, v_ref, o_ref, km_ref, *, nb):
    i = pl.program_id(1)
    blk = MOBA_BLOCK
    scale = HEAD_DIM ** -0.5

    @pl.when(i == 0)
    def _():
        km_ref[...] = jnp.zeros_like(km_ref)
        km_ref[0:nb, :] = jnp.mean(k_ref[...].reshape(nb, blk, HEAD_DIM), axis=1)

    q = q_ref[...]
    gate = _dot_nt(q, km_ref[...], HI)
    col = lax.broadcasted_iota(jnp.int32, gate.shape, 1)
    sel = _topk_mask(gate, col < i, 1)
    qb = q.astype(BF16)

    row0 = pl.multiple_of(i * blk, blk)
    s = _dot_nt(qb, k_ref[pl.ds(row0, blk), :].astype(BF16)) * scale
    r_id = lax.broadcasted_iota(jnp.int32, s.shape, 0)
    c_id = lax.broadcasted_iota(jnp.int32, s.shape, 1)
    s = jnp.where(c_id <= r_id, s, NEG_INF)
    m0 = jnp.max(s, axis=1, keepdims=True)
    p = jnp.exp(s - m0)
    l0 = jnp.sum(p, axis=1, keepdims=True)
    acc0 = _dot(p.astype(BF16), v_ref[pl.ds(row0, blk), :].astype(BF16))

    def body(j, carry):
        m, l, acc = carry
        rows = pl.ds(pl.multiple_of(j * blk, blk), blk)
        picked = jnp.max(jnp.where(col == j, sel, 0.0), axis=1, keepdims=True)
        s = _dot_nt(qb, k_ref[rows, :].astype(BF16)) * scale
        s = jnp.where(picked > 0.0, s, NEG_INF)
        m_new = jnp.maximum(m, jnp.max(s, axis=1, keepdims=True))
        p = jnp.exp(s - m_new)
        alpha = jnp.exp(m - m_new)
        l = alpha * l + jnp.sum(p, axis=1, keepdims=True)
        acc = alpha * acc + _dot(p.astype(BF16), v_ref[rows, :].astype(BF16))
        return m_new, l, acc

    _, l, acc = lax.fori_loop(0, i, body, (m0, l0, acc0))
    o_ref[...] = acc / l


def _moba_prompt(q, k, v):
    t, w = q.shape
    nh = w // HEAD_DIM
    assert t % MOBA_BLOCK == 0 and t // MOBA_BLOCK <= LANE
    nb = t // MOBA_BLOCK
    head = pl.BlockSpec((t, HEAD_DIM), lambda h, i: (0, h))
    qblk = pl.BlockSpec((MOBA_BLOCK, HEAD_DIM), lambda h, i: (i, h))
    return pl.pallas_call(
        functools.partial(_moba_prompt_kernel, nb=nb),
        grid=(nh, nb),
        in_specs=[qblk, head, head],
        out_specs=qblk,
        out_shape=jax.ShapeDtypeStruct((t, w), F32),
        scratch_shapes=[pltpu.VMEM((LANE, HEAD_DIM), F32)],
        compiler_params=_params("parallel", "arbitrary"),
        name="moba_prompt",
    )(q, k, v)


def _silu(x):
    return x * jax.nn.sigmoid(x)


def _softplus(x):
    return jnp.maximum(x, 0.0) + jnp.log1p(jnp.exp(-jnp.abs(x)))


def _conv_kernel(x_ref, halo_ref, prev_ref, w_ref, o_ref, *, nh):
    i = pl.program_id(0)
    j = pl.program_id(1)
    tm = x_ref.shape[0]
    halo = jnp.where(i == 0, prev_ref[...], halo_ref[...])
    xp = jnp.concatenate([halo, x_ref[...]], axis=0)
    w = w_ref[...]
    base = SUBLANE - (CONV_WIDTH - 1)
    conv = sum(xp[base + c:base + c + tm, :] * w[c:c + 1, :] for c in range(CONV_WIDTH))
    act = _silu(conv)
    inv = lax.rsqrt(jnp.sum(act * act, axis=-1, keepdims=True) + L2_EPS)
    scale = jnp.where(j < nh, inv * (HEAD_DIM ** -0.5), jnp.where(j < 2 * nh, inv, 1.0))
    o_ref[...] = act * scale


def _conv_prompt(raw, conv_prev, conv_w):
    t, cw = raw.shape
    nh = cw // (3 * HEAD_DIM)
    tm = _tile(t, 512, SUBLANE)
    pad = SUBLANE - (CONV_WIDTH - 1)
    prev = jnp.pad(conv_prev.astype(F32), ((pad, 0), (0, 0)))
    wpad = jnp.pad(conv_w, ((0, SUBLANE - CONV_WIDTH), (0, 0)))
    rows_per = tm // SUBLANE
    return pl.pallas_call(
        functools.partial(_conv_kernel, nh=nh),
        grid=(t // tm, cw // HEAD_DIM),
        in_specs=[
            pl.BlockSpec((tm, HEAD_DIM), lambda i, j: (i, j)),
            pl.BlockSpec((SUBLANE, HEAD_DIM), lambda i, j: (jnp.maximum(i * rows_per - 1, 0), j)),
            pl.BlockSpec((SUBLANE, HEAD_DIM), lambda i, j: (0, j)),
            pl.BlockSpec((SUBLANE, HEAD_DIM), lambda i, j: (0, j)),
        ],
        out_specs=pl.BlockSpec((tm, HEAD_DIM), lambda i, j: (i, j)),
        out_shape=jax.ShapeDtypeStruct((t, cw), F32),
        compiler_params=_params("parallel", "arbitrary"),
        name="conv_prompt",
    )(raw, raw, prev, wpad)


def _delta_prompt_kernel(q_ref, k_ref, v_ref, z_ref, ab_ref, alog_ref, dtb_ref, og_ref, o_ref, s_ref, *, hb):
    c = pl.program_id(1)
    n = DELTA_CHUNK

    @pl.when(c == 0)
    def _():
        s_ref[...] = jnp.zeros_like(s_ref)

    ab = ab_ref[...]
    g_all = -jnp.exp(alog_ref[...]) * _softplus(ab + dtb_ref[...])
    beta_all = jax.nn.sigmoid(ab)
    r_id = lax.broadcasted_iota(jnp.int32, (n, n), 0)
    c_id = lax.broadcasted_iota(jnp.int32, (n, n), 1)
    incl = r_id >= c_id
    strict = r_id > c_id
    eye = jnp.where(r_id == c_id, 1.0, 0.0)
    gcum = _dot(jnp.where(incl, 1.0, 0.0), g_all, HI)
    gcum_t = gcum.T
    og = og_ref[...]

    for hh in range(hb):
        cols = slice(hh * HEAD_DIM, (hh + 1) * HEAD_DIM)
        q, k, v = q_ref[:, cols], k_ref[:, cols], v_ref[:, cols]
        gc = gcum[:, hh:hh + 1]
        gr = gcum_t[hh:hh + 1, :]
        bc = beta_all[:, hb + hh:hb + hh + 1]
        decay = jnp.where(incl, jnp.exp(jnp.minimum(gc - gr, 0.0)), 0.0)
        low = jnp.where(strict, _dot_nt(k, k, HI) * bc * decay, 0.0)
        qk = _dot_nt(q, k, HI) * decay
        inv = eye - low
        pw = low
        for _ in range(int(math.log2(n)) - 1):
            pw = _dot(pw, pw, HI)
            inv = inv + _dot(pw, inv, HI)
        eg = jnp.exp(gc)
        uw = _dot(inv, jnp.concatenate([v * bc, k * (bc * eg)], axis=1), HI)
        u, w = uw[:, :HEAD_DIM], uw[:, HEAD_DIM:]
        s = s_ref[hh]
        v_new = u - _dot(w, s, HI)
        o = _dot(q * eg, s, HI) + _dot(qk, v_new, HI)
        g_last = gc[n - 1:n, :]
        s_ref[hh] = s * jnp.exp(g_last) + _dot_tn(k * jnp.exp(g_last - gc), v_new, HI)
        o_ref[:, cols] = _rms(o, og) * _silu(z_ref[:, cols])


def _delta_prompt(act, z, ab, alog, dtb, o_gain, hb):
    t, cw = act.shape
    nh = cw // (3 * HEAD_DIM)
    ng = nh // hb
    assert t % DELTA_CHUNK == 0
    n = DELTA_CHUNK
    gw = hb * HEAD_DIM
    grp = lambda off: pl.BlockSpec((n, gw), lambda g, c: (c, g + off))
    vec = pl.BlockSpec((1, LANE), lambda g, c: (0, g))
    return pl.pallas_call(
        functools.partial(_delta_prompt_kernel, hb=hb),
        grid=(ng, t // n),
        in_specs=[grp(0), grp(ng), grp(2 * ng), grp(0),
                  pl.BlockSpec((n, LANE), lambda g, c: (c, g)), vec, vec,
                  pl.BlockSpec((1, HEAD_DIM), lambda g, c: (0, 0))],
        out_specs=[grp(0), pl.BlockSpec((hb, HEAD_DIM, HEAD_DIM), lambda g, c: (g, 0, 0))],
        out_shape=[jax.ShapeDtypeStruct((t, nh * HEAD_DIM), F32),
                   jax.ShapeDtypeStruct((nh, HEAD_DIM, HEAD_DIM), F32)],
        compiler_params=_params("parallel", "arbitrary"),
        name="delta_prompt",
    )(act, act, act, z, ab, alog, dtb, o_gain)


def _page_gate_kernel(pt_ref, ck_ref, q_ref, sel_ref, bm_ref, *, n_pages):
    p = pl.program_id(1)
    ppb = MOBA_BLOCK // PAGE_SIZE
    s = jnp.sum(ck_ref[0, 0], axis=0)
    blk = p // ppb

    @pl.when(p % ppb == 0)
    def _():
        bm_ref[blk] = s

    @pl.when(p % ppb != 0)
    def _():
        bm_ref[blk] = bm_ref[blk] + s

    @pl.when(p == n_pages - 1)
    def _():
        bm = bm_ref[...] * (1.0 / MOBA_BLOCK)
        gate = jnp.sum(bm * q_ref[...], axis=-1, keepdims=True)
        gate = jnp.broadcast_to(gate, bm.shape)
        idx = lax.broadcasted_iota(jnp.int32, gate.shape, 0)
        nb = gate.shape[0]
        for r in range(MOBA_TOPK):
            top = jnp.max(gate, axis=0, keepdims=True)
            first = jnp.min(jnp.where(gate == top, idx, nb), axis=0, keepdims=True)
            sel_ref[0, r] = first[0]
            gate = jnp.where(idx == first, -jnp.inf, gate)


def _page_gate(page_table, cache_k, q):
    b, n_pages = page_table.shape
    _, _, page, nh, dh = cache_k.shape
    ppb = MOBA_BLOCK // PAGE_SIZE
    assert page == PAGE_SIZE and dh == HEAD_DIM and n_pages % ppb == 0 and n_pages // ppb >= MOBA_TOPK
    grid_spec = pltpu.PrefetchScalarGridSpec(
        num_scalar_prefetch=1,
        grid=(b, n_pages),
        in_specs=[
            pl.BlockSpec((1, 1, page, nh, dh), lambda i, p, pt: (0, pt[i, p], 0, 0, 0)),
            pl.BlockSpec((1, nh, dh), lambda i, p, pt: (i, 0, 0)),
        ],
        out_specs=pl.BlockSpec((1, MOBA_TOPK, nh, dh), lambda i, p, pt: (i, 0, 0, 0)),
        scratch_shapes=[pltpu.VMEM((n_pages // ppb, nh, dh), F32)],
    )
    return pl.pallas_call(
        functools.partial(_page_gate_kernel, n_pages=n_pages),
        grid_spec=grid_spec,
        out_shape=jax.ShapeDtypeStruct((b, MOBA_TOPK, nh, dh), jnp.int32),
        compiler_params=_params("parallel", "arbitrary"),
        name="page_gate",
    )(page_table, cache_k, q)


def _paged_attn_kernel(pt_ref, sel_ref, q_ref, kn_ref, vn_ref, *refs, n_slab, hg):
    k_refs = refs[:n_slab]
    v_refs = refs[n_slab:2 * n_slab]
    o_ref = refs[2 * n_slab]
    h = pl.program_id(1)
    scale = HEAD_DIM ** -0.5
    q = q_ref[...].reshape(1, 1, HEAD_DIM)
    sub = lax.broadcasted_iota(jnp.int32, (PAGE_SIZE, hg, 1), 1)
    mine = sub == (h % hg)
    s_own = jnp.sum(q_ref[...] * kn_ref[...], axis=-1, keepdims=True).reshape(1, 1, 1) * scale
    logits = []
    for kr in k_refs:
        s = jnp.sum(kr[0, 0] * q, axis=-1, keepdims=True) * scale
        logits.append(jnp.where(mine, s, NEG_INF))
    m = s_own
    for s in logits:
        m = jnp.maximum(m, jnp.max(s, axis=(0, 1), keepdims=True))
    p_own = jnp.exp(s_own - m)
    l = p_own
    acc = p_own.reshape(1, 1) * vn_ref[...].reshape(1, HEAD_DIM)
    for s, vr in zip(logits, v_refs):
        p = jnp.exp(s - m)
        l = l + jnp.sum(p, axis=(0, 1), keepdims=True)
        acc = acc + jnp.sum(p * vr[0, 0], axis=(0, 1)).reshape(1, HEAD_DIM)
    o_ref[...] = (acc / l.reshape(1, 1)).reshape(o_ref.shape)


def _paged_attn(page_table, sel, q, k_new, v_new, cache_k, cache_v):
    b = q.shape[0]
    _, _, page, nh, dh = cache_k.shape
    ppb = MOBA_BLOCK // PAGE_SIZE
    hg = SUBLANE
    assert nh % hg == 0
    n_slab = MOBA_TOPK * ppb

    def slab(s, r):
        return pl.BlockSpec(
            (1, 1, page, hg, dh),
            lambda i, h, pt, sl: (0, pt[i, ppb * sl[i, s * nh + h] + r], 0, h // hg, 0))

    slabs = [slab(s, r) for s in range(MOBA_TOPK) for r in range(ppb)]
    vec = pl.BlockSpec((1, 1, dh), lambda i, h, pt, sl: (i, 0, h))
    grid_spec = pltpu.PrefetchScalarGridSpec(
        num_scalar_prefetch=2,
        grid=(b, nh),
        in_specs=[vec, vec, vec] + slabs + slabs,
        out_specs=vec,
    )
    return pl.pallas_call(
        functools.partial(_paged_attn_kernel, n_slab=n_slab, hg=hg),
        grid_spec=grid_spec,
        out_shape=jax.ShapeDtypeStruct((b, 1, nh * dh), F32),
        compiler_params=_params("parallel", "arbitrary"),
        name="paged_attn",
    )(page_table, sel, q, k_new, v_new, *([cache_k] * n_slab), *([cache_v] * n_slab))


def _delta_step_kernel(raw_ref, conv_ref, w_ref, z_ref, ab_ref, alog_ref, dtb_ref, og_ref, s_ref,
                       o_ref, convn_ref, sn_ref, *, nh):
    raw = raw_ref[0]
    prev = conv_ref[0]
    w = w_ref[...]
    conv = raw * w[CONV_WIDTH - 1]
    for c in range(CONV_WIDTH - 1):
        conv = conv + prev[c] * w[c]
    for c in range(CONV_WIDTH - 2):
        convn_ref[0, c] = prev[c + 1]
    convn_ref[0, CONV_WIDTH - 2] = raw
    act = _silu(conv)
    inv = lax.rsqrt(jnp.sum(act * act, axis=-1, keepdims=True) + L2_EPS)
    q = act[0:nh] * inv[0:nh] * (HEAD_DIM ** -0.5)
    k = act[nh:2 * nh] * inv[nh:2 * nh]
    v = act[2 * nh:3 * nh]
    ab = ab_ref[0]
    g = -jnp.exp(alog_ref[...]) * _softplus(ab + dtb_ref[...])
    beta = jax.nn.sigmoid(ab)
    pad = jnp.zeros((HEAD_DIM - nh, HEAD_DIM), F32)
    k_t = jnp.concatenate([k, pad], axis=0).T
    q_t = jnp.concatenate([q, pad], axis=0).T
    outs = []
    for h in range(nh):
        eg = jnp.exp(g[:, h:h + 1])
        bh = beta[:, nh + h:nh + h + 1]
        s = s_ref[0, h] * eg
        kc = k_t[:, h:h + 1]
        v_new = (v[h:h + 1, :] - jnp.sum(kc * s, axis=0, keepdims=True)) * bh
        s = s + kc * v_new
        sn_ref[0, h] = s
        outs.append(jnp.sum(q_t[:, h:h + 1] * s, axis=0, keepdims=True))
    o = jnp.concatenate(outs, axis=0)
    o_ref[0] = _rms(o, og_ref[...]) * _silu(z_ref[0])


def _delta_step(raw, state_conv, conv_w, z, ab, alog, dtb, o_gain, state):
    b, rows, dh = raw.shape
    nh = rows // 3
    assert 2 * nh <= LANE
    bs = lambda *shape: pl.BlockSpec((1,) + shape, lambda i: (i,) + (0,) * len(shape))
    const = lambda *shape: pl.BlockSpec(shape, lambda i: (0,) * len(shape))
    return pl.pallas_call(
        functools.partial(_delta_step_kernel, nh=nh),
        grid=(b,),
        in_specs=[bs(rows, dh), bs(CONV_WIDTH - 1, rows, dh), const(CONV_WIDTH, rows, dh), bs(nh, dh),
                  bs(1, LANE), const(1, LANE), const(1, LANE), const(1, dh), bs(nh, dh, dh)],
        out_specs=[bs(nh, dh), bs(CONV_WIDTH - 1, rows, dh), bs(nh, dh, dh)],
        out_shape=[jax.ShapeDtypeStruct((b, nh, dh), F32),
                   jax.ShapeDtypeStruct((b, CONV_WIDTH - 1, rows, dh), F32),
                   jax.ShapeDtypeStruct((b, nh, dh, dh), F32)],
        compiler_params=_params("parallel"),
        name="delta_step",
    )(raw, state_conv, conv_w, z, ab, alog, dtb, o_gain, state)


def _pad_lanes(x):
    return jnp.pad(x, ((0, 0), (0, LANE - x.shape[1])))


def kernel(x_prompt, x_sample, cache_k, cache_v, state_conv, state_delta, page_table, p_prompt, p_sample,
           ffn1_norm, ffn1_w_up, ffn1_w_down, mix_norm, w_in, q_norm, k_norm, conv_w, a_log, dt_bias,
           o_norm, w_out, ffn2_norm, ffn2_w_up, ffn2_w_down, ple_norm, ple_gate, ple_proj):
    depth = ffn1_norm.shape[0]
    assert depth == 1, "one trunk layer"
    bp, t, d = x_prompt.shape
    bs, ts, _ = x_sample.shape
    assert bp == 1 and ts == 1
    nh = d // HEAD_DIM
    hb = 4
    assert nh % hb == 0 and 2 * hb <= LANE
    ng = nh // hb
    past_len = page_table.shape[1] * PAGE_SIZE
    assert past_len % MOBA_BLOCK == 0

    w = w_in[0]
    o_qk, o_v, o_b, o_z, o_a, o_bb, o_g = 0, 2 * d, 3 * d, 6 * d, 7 * d, 7 * d + nh, 7 * d + 2 * nh
    w_qk = w[:, o_qk:o_v].astype(BF16)
    w_v = w[:, o_v:o_b].astype(BF16)
    w_b = w[:, o_b:o_z].astype(BF16)
    w_z = w[:, o_z:o_a].astype(BF16)
    w_gates = w[:, o_g:o_g + 2 * d].astype(BF16)
    w_a, w_bb = w[:, o_a:o_bb], w[:, o_bb:o_g]
    w_ab_s = _pad_lanes(jnp.concatenate([w_a, w_bb], axis=1)).astype(BF16)
    w_ab_p = jnp.concatenate(
        [_pad_lanes(jnp.concatenate([w_a[:, g * hb:(g + 1) * hb], w_bb[:, g * hb:(g + 1) * hb]], axis=1))
         for g in range(ng)], axis=1).astype(BF16)
    grp_vec = lambda x: jnp.concatenate([_pad_lanes(x[:, g * hb:(g + 1) * hb]) for g in range(ng)], axis=1)
    alog_p, dtb_p = grp_vec(a_log.astype(F32)), grp_vec(dt_bias.astype(F32))
    alog_s, dtb_s = _pad_lanes(a_log.astype(F32)), _pad_lanes(dt_bias.astype(F32))
    up1, down1 = ffn1_w_up[0].astype(BF16), ffn1_w_down[0].astype(BF16)
    up2, down2 = ffn2_w_up[0].astype(BF16), ffn2_w_down[0].astype(BF16)
    wo = w_out[0].astype(BF16)
    wpg, wpp = ple_gate[0].astype(BF16), ple_proj[0].astype(BF16)
    cw = conv_w[0].astype(F32)

    def trunk_in(x):
        h = _rmsnorm(x, ffn1_norm)
        x1, h1 = _ffn(x, h, up1, down1, mix_norm)
        return x1, h1

    def trunk_out(x1, o_a_, o_b_, gates, pe):
        x2, h2 = _outproj(x1, o_a_, o_b_, gates, wo, ffn2_norm)
        x3, h3 = _ffn(x2, h2, up2, down2, ple_norm)
        return _ple(x3, h3, pe.astype(BF16), wpg, wpp)

    xp = x_prompt[0]
    x1, h1 = trunk_in(xp)
    cos_p, sin_p = _rope_tables(jnp.arange(t, dtype=jnp.int32))
    q_p, k_p = _rope(_matmul(h1, w_qk), q_norm, k_norm, cos_p, sin_p)
    v_p = _matmul(h1, w_v)
    raw_p = _matmul(h1, w_b)
    z_p = _matmul(h1, w_z)
    ab_p = _matmul(h1, w_ab_p)
    gates_p = _matmul(h1, w_gates)
    oa_p = _moba_prompt(q_p, k_p, v_p)
    conv0 = jnp.zeros((CONV_WIDTH - 1, 3 * d), F32)
    act_p = _conv_prompt(raw_p, conv0, cw)
    ob_p, delta_p = _delta_prompt(act_p, z_p, ab_p, alog_p, dtb_p, o_norm, hb)
    y_p = trunk_out(x1, oa_p, ob_p, gates_p, p_prompt[0, 0])
    conv_p = jnp.concatenate([conv0, raw_p], axis=0)[t:]

    xs = x_sample[:, 0]
    x1s, h1s = trunk_in(xs)
    pos_s = jnp.full((bs,), past_len, jnp.int32)
    cos_s, sin_s = _rope_tables(pos_s)
    q_s, k_s = _rope(_matmul(h1s, w_qk), q_norm, k_norm, cos_s, sin_s)
    v_s = _matmul(h1s, w_v)
    raw_s = _matmul(h1s, w_b)
    z_s = _matmul(h1s, w_z)
    ab_s = _matmul(h1s, w_ab_s)
    gates_s = _matmul(h1s, w_gates)
    sel = _page_gate(page_table, cache_k, q_s.reshape(bs, nh, HEAD_DIM))
    sel = sel[:, :, :, 0].reshape(bs, MOBA_TOPK * nh)
    row3 = lambda x: x.reshape(bs, 1, nh * HEAD_DIM)
    oa_s = _paged_attn(page_table, sel, row3(q_s), row3(k_s), row3(v_s), cache_k, cache_v)[:, 0]
    ob_s, conv_s, delta_s = _delta_step(
        raw_s.reshape(bs, 3 * nh, HEAD_DIM),
        state_conv[0].astype(F32).reshape(bs, CONV_WIDTH - 1, 3 * nh, HEAD_DIM),
        cw.reshape(CONV_WIDTH, 3 * nh, HEAD_DIM),
        z_s.reshape(bs, nh, HEAD_DIM), ab_s.reshape(bs, 1, LANE), alog_s, dtb_s, o_norm,
        state_delta[0].astype(F32))
    y_s = trunk_out(x1s, oa_s, ob_s.reshape(bs, d), gates_s, p_sample[0, :, 0])

    heads = lambda x, n: x.reshape(1, n, -1, nh, HEAD_DIM)
    return (y_p[None], y_s[:, None],
            heads(k_p, 1), heads(v_p, 1), conv_p[None, None], delta_p[None, None],
            heads(k_s, bs), heads(v_s, bs), conv_s.reshape(1, bs, CONV_WIDTH - 1, 3 * d), delta_s[None])
```

```python
import functools
import math

import jax
import jax.numpy as jnp
from jax import lax
from jax.experimental import pallas as pl
from jax.experimental.pallas import tpu as pltpu

F32 = jnp.float32
BF16 = jnp.bfloat16
HI = lax.Precision.HIGHEST

HEAD_DIM = 128
MOBA_BLOCK = 256
MOBA_TOPK = 3
PAGE_SIZE = 128
CONV_WIDTH = 4
DELTA_CHUNK = 64
ROPE_THETA = 10000.0
RMS_EPS = 1e-6
L2_EPS = 1e-6
NEG_INF = -1e30

LANE = 128
SUBLANE = 8
VMEM_LIMIT_BYTES = 56 * 1024 * 1024


def _params(*sem):
    return pltpu.CompilerParams(dimension_semantics=sem, vmem_limit_bytes=VMEM_LIMIT_BYTES)


def _tile(n, pref, quantum):
    if n <= pref:
        return n
    t = (pref // quantum) * quantum
    while t > quantum and n % t:
        t -= quantum
    assert n % t == 0, (n, pref, quantum)
    return t


def _rms(x, gain):
    return x * lax.rsqrt(jnp.mean(x * x, axis=-1, keepdims=True) + RMS_EPS) * gain


def _dot(a, b, precision=None):
    return jnp.dot(a, b, preferred_element_type=F32, precision=precision)


def _dot_nt(a, b, precision=None):
    return lax.dot_general(a, b, (((1,), (1,)), ((), ())), preferred_element_type=F32, precision=precision)


def _dot_tn(a, b, precision=None):
    return lax.dot_general(a, b, (((0,), (0,)), ((), ())), preferred_element_type=F32, precision=precision)


def _split(x):
    hi = x.astype(BF16)
    return hi, (x - hi.astype(F32)).astype(BF16)


def _dot3_each(a_list, b_list, dot=_dot):
    a_parts = [_split(a) for a in a_list]
    b_parts = [_split(b) for b in b_list]
    passes = [(dot(a_hi, b_hi), dot(a_hi, b_lo), dot(a_lo, b_hi))
              for (a_hi, a_lo), (b_hi, b_lo) in zip(a_parts, b_parts)]
    return [hh + hl + lh for hh, hl, lh in passes]


def _rmsnorm_kernel(x_ref, g_ref, o_ref):
    o_ref[...] = _rms(x_ref[...], g_ref[...]).astype(o_ref.dtype)


def _rmsnorm(x, gain):
    m, d = x.shape
    tm = _tile(m, 512, SUBLANE)
    return pl.pallas_call(
        _rmsnorm_kernel,
        grid=(m // tm,),
        in_specs=[pl.BlockSpec((tm, d), lambda i: (i, 0)), pl.BlockSpec((1, d), lambda i: (0, 0))],
        out_specs=pl.BlockSpec((tm, d), lambda i: (i, 0)),
        out_shape=jax.ShapeDtypeStruct((m, d), BF16),
        compiler_params=_params("parallel"),
        name="rmsnorm",
    )(x, gain)


def _ffn_kernel(x_ref, h_ref, wg_ref, wu_ref, wd_ref, ng_ref, y_ref, hn_ref, acc_ref, *, nf):
    f = pl.program_id(1)

    @pl.when(f == 0)
    def _():
        acc_ref[...] = jnp.zeros_like(acc_ref)

    h = h_ref[...]
    g = _dot(h, wg_ref[...])
    u = _dot(h, wu_ref[...])
    a = (g * jax.nn.sigmoid(g) * u).astype(BF16)
    acc_ref[...] += _dot(a, wd_ref[...])

    @pl.when(f == nf - 1)
    def _():
        y = x_ref[...] + 0.5 * acc_ref[...]
        y_ref[...] = y
        hn_ref[...] = _rms(y, ng_ref[...]).astype(hn_ref.dtype)


def _ffn(x, h, w_up, w_down, next_gain):
    m, d = x.shape
    ff = w_down.shape[0]
    tm = _tile(m, 512, SUBLANE)
    tf = _tile(ff, 512, LANE)
    nf = ff // tf
    return pl.pallas_call(
        functools.partial(_ffn_kernel, nf=nf),
        grid=(m // tm, nf),
        in_specs=[
            pl.BlockSpec((tm, d), lambda i, f: (i, 0)),
            pl.BlockSpec((tm, d), lambda i, f: (i, 0)),
            pl.BlockSpec((d, tf), lambda i, f: (0, f)),
            pl.BlockSpec((d, tf), lambda i, f: (0, f + nf)),
            pl.BlockSpec((tf, d), lambda i, f: (f, 0)),
            pl.BlockSpec((1, d), lambda i, f: (0, 0)),
        ],
        out_specs=[pl.BlockSpec((tm, d), lambda i, f: (i, 0)), pl.BlockSpec((tm, d), lambda i, f: (i, 0))],
        out_shape=[jax.ShapeDtypeStruct((m, d), F32), jax.ShapeDtypeStruct((m, d), BF16)],
        scratch_shapes=[pltpu.VMEM((tm, d), F32)],
        compiler_params=_params("parallel", "arbitrary"),
        name="ffn",
    )(x, h, w_up, w_up, w_down, next_gain)


def _mm_kernel(h_ref, w_ref, o_ref):
    o_ref[...] = _dot(h_ref[...], w_ref[...])


def _matmul(h, w):
    m, kd = h.shape
    n = w.shape[1]
    tm = _tile(m, 512, SUBLANE)
    tn = _tile(n, 1024, LANE)
    return pl.pallas_call(
        _mm_kernel,
        grid=(m // tm, n // tn),
        in_specs=[pl.BlockSpec((tm, kd), lambda i, j: (i, 0)), pl.BlockSpec((kd, tn), lambda i, j: (0, j))],
        out_specs=pl.BlockSpec((tm, tn), lambda i, j: (i, j)),
        out_shape=jax.ShapeDtypeStruct((m, n), F32),
        compiler_params=_params("parallel", "arbitrary"),
        name="proj",
    )(h, w)


def _outproj_kernel(x_ref, oa_ref, ob_ref, ga_ref, gb_ref, w_ref, ng_ref, y_ref, hn_ref):
    merged = jax.nn.sigmoid(ga_ref[...]) * oa_ref[...] + jax.nn.sigmoid(gb_ref[...]) * ob_ref[...]
    y = x_ref[...] + _dot(merged.astype(BF16), w_ref[...])
    y_ref[...] = y
    hn_ref[...] = _rms(y, ng_ref[...]).astype(hn_ref.dtype)


def _outproj(x, o_a, o_b, gates, w_out, next_gain):
    m, d = x.shape
    tm = _tile(m, 256, SUBLANE)
    row = lambda i: (i, 0)
    return pl.pallas_call(
        _outproj_kernel,
        grid=(m // tm,),
        in_specs=[
            pl.BlockSpec((tm, d), row),
            pl.BlockSpec((tm, d), row),
            pl.BlockSpec((tm, d), row),
            pl.BlockSpec((tm, d), lambda i: (i, 0)),
            pl.BlockSpec((tm, d), lambda i: (i, 1)),
            pl.BlockSpec((d, d), lambda i: (0, 0)),
            pl.BlockSpec((1, d), lambda i: (0, 0)),
        ],
        out_specs=[pl.BlockSpec((tm, d), row), pl.BlockSpec((tm, d), row)],
        out_shape=[jax.ShapeDtypeStruct((m, d), F32), jax.ShapeDtypeStruct((m, d), BF16)],
        compiler_params=_params("parallel"),
        name="outproj",
    )(x, o_a, o_b, gates, gates, w_out, next_gain)


def _ple_kernel(x_ref, h_ref, pe_ref, wg_ref, wp_ref, y_ref):
    gate = jax.nn.sigmoid(_dot(h_ref[...], wg_ref[...]))
    y_ref[...] = x_ref[...] + gate * _dot(pe_ref[...], wp_ref[...])


def _ple(x, h, pe, w_gate, w_proj):
    m, d = x.shape
    pd = pe.shape[1]
    tm = _tile(m, 512, SUBLANE)
    tn = _tile(d, 1024, LANE)
    return pl.pallas_call(
        _ple_kernel,
        grid=(m // tm, d // tn),
        in_specs=[
            pl.BlockSpec((tm, tn), lambda i, j: (i, j)),
            pl.BlockSpec((tm, d), lambda i, j: (i, 0)),
            pl.BlockSpec((tm, pd), lambda i, j: (i, 0)),
            pl.BlockSpec((d, tn), lambda i, j: (0, j)),
            pl.BlockSpec((pd, tn), lambda i, j: (0, j)),
        ],
        out_specs=pl.BlockSpec((tm, tn), lambda i, j: (i, j)),
        out_shape=jax.ShapeDtypeStruct((m, d), F32),
        compiler_params=_params("parallel", "arbitrary"),
        name="ple",
    )(x, h, pe, w_gate, w_proj)


def _rope_kernel(xq_ref, xk_ref, gq_ref, gk_ref, c_ref, s_ref, q_ref, k_ref, *prompt_refs, for_prompt):
    c = c_ref[...]
    s = s_ref[...]

    def norm_rot(x, gain):
        y = _rms(x, gain)
        return y * c + pltpu.roll(y, HEAD_DIM // 2, 1) * s

    q = norm_rot(xq_ref[...], gq_ref[...])
    k = norm_rot(xk_ref[...], gk_ref[...])
    k_ref[...] = k
    if for_prompt:
        kb_ref, km_ref = prompt_refs
        q_ref[...] = q.T
        kb_ref[...] = k.astype(kb_ref.dtype)
        km_ref[...] = jnp.mean(k.reshape(km_ref.shape[0], MOBA_BLOCK, HEAD_DIM), axis=1)
    else:
        q_ref[...] = q


def _rope(qk_raw, q_gain, k_gain, cos_tab, sin_tab, for_prompt):
    m, w2 = qk_raw.shape
    nh = w2 // (2 * HEAD_DIM)
    tm = _tile(m, SUBLANE * MOBA_BLOCK, SUBLANE * MOBA_BLOCK) if for_prompt else _tile(m, 1024, SUBLANE)
    blk = lambda off: pl.BlockSpec((tm, HEAD_DIM), lambda i, h: (i, h + off))
    vec = pl.BlockSpec((1, HEAD_DIM), lambda i, h: (0, 0))
    tab = pl.BlockSpec((tm, HEAD_DIM), lambda i, h: (i, 0))
    out = jax.ShapeDtypeStruct((m, nh * HEAD_DIM), F32)
    if for_prompt:
        assert m % tm == 0 and tm % (SUBLANE * MOBA_BLOCK) == 0
        out_specs = [pl.BlockSpec((HEAD_DIM, tm), lambda i, h: (h, i)), blk(0), blk(0),
                     pl.BlockSpec((tm // MOBA_BLOCK, HEAD_DIM), lambda i, h: (i, h))]
        out_shape = [jax.ShapeDtypeStruct((nh * HEAD_DIM, m), F32), out,
                     jax.ShapeDtypeStruct((m, nh * HEAD_DIM), BF16),
                     jax.ShapeDtypeStruct((m // MOBA_BLOCK, nh * HEAD_DIM), F32)]
    else:
        out_specs, out_shape = [blk(0), blk(0)], [out, out]
    return pl.pallas_call(
        functools.partial(_rope_kernel, for_prompt=for_prompt),
        grid=(m // tm, nh),
        in_specs=[blk(0), blk(nh), vec, vec, tab, tab],
        out_specs=out_specs,
        out_shape=out_shape,
        compiler_params=_params("parallel", "arbitrary"),
        name="rope",
    )(qk_raw, qk_raw, q_gain, k_gain, cos_tab, sin_tab)


def _rope_tables(pos):
    half = HEAD_DIM // 2
    inv_freq = ROPE_THETA ** (-jnp.arange(half, dtype=F32) / half)
    ang = pos.astype(F32)[:, None] * inv_freq[None, :]
    cos, sin = jnp.cos(ang), jnp.sin(ang)
    return jnp.concatenate([cos, cos], axis=1), jnp.concatenate([-sin, sin], axis=1)


def _topk_mask(gate, valid, axis):
    n = gate.shape[axis]
    idx = lax.broadcasted_iota(jnp.int32, gate.shape, axis)
    g = jnp.where(valid, gate, NEG_INF)
    sel = jnp.zeros(gate.shape, F32)
    for _ in range(MOBA_TOPK):
        top = jnp.max(g, axis=axis, keepdims=True)
        first = jnp.min(jnp.where(g == top, idx, n), axis=axis, keepdims=True)
        hit = idx == first
        sel = jnp.where(hit, jnp.where(valid, 1.0, 0.0), sel)
        g = jnp.where(hit, -jnp.inf, g)
    return sel


MOBA_HEADS = 4


def _moba_prompt_kernel(qt_ref, km_ref, kb_ref, vt_ref, o_ref, sel_ref, acc_ref, *, hp):
    i = pl.program_id(1)
    blk = MOBA_BLOCK
    q_scale = HEAD_DIM ** -0.5 * math.log2(math.e)
    heads = [slice(c * HEAD_DIM, (c + 1) * HEAD_DIM) for c in range(hp)]
    row0 = pl.multiple_of(i * blk, blk)
    k_id = lax.broadcasted_iota(jnp.int32, (blk, blk), 0)
    q_id = lax.broadcasted_iota(jnp.int32, (blk, blk), 1)

    qs, m0, l0 = [], [], []
    for c, hs in enumerate(heads):
        qt = qt_ref[hs, :]
        gate = _dot(km_ref[:, hs], qt, HI)
        bid = lax.broadcasted_iota(jnp.int32, gate.shape, 0)
        sel_ref[c] = _topk_mask(gate, bid < i, 0)
        q_c = (qt * q_scale).astype(BF16)
        s = jnp.where(k_id <= q_id, _dot(kb_ref[pl.ds(row0, blk), hs], q_c), NEG_INF)
        m_c = jnp.max(s, axis=0, keepdims=True)
        p = jnp.exp2(s - m_c)
        acc_ref[c] = _dot(vt_ref[hs, pl.ds(row0, blk)], p.astype(BF16))
        qs.append(q_c)
        m0.append(m_c)
        l0.append(jnp.sum(p, axis=0, keepdims=True))

    def body(j, carry):
        ms, ls = carry
        r0 = pl.multiple_of(j * blk, blk)
        scores = [_dot(kb_ref[pl.ds(r0, blk), hs], qs[c]) for c, hs in enumerate(heads)]
        new_m, new_l, alphas, pvs = [], [], [], []
        for c, hs in enumerate(heads):
            pick = sel_ref[c, pl.ds(j, 1), :]
            s = jnp.where(pick > 0.0, scores[c], NEG_INF)
            m_new = jnp.maximum(ms[c], jnp.max(s, axis=0, keepdims=True))
            p = jnp.exp2(s - m_new)
            alpha = jnp.exp2(ms[c] - m_new)
            new_l.append(alpha * ls[c] + jnp.sum(p, axis=0, keepdims=True))
            pvs.append(_dot(vt_ref[hs, pl.ds(r0, blk)], p.astype(BF16)))
            alphas.append(alpha)
            new_m.append(m_new)
        for c in range(hp):
            acc_ref[c] = alphas[c] * acc_ref[c] + pvs[c]
        return tuple(new_m), tuple(new_l)

    _, ls = lax.fori_loop(0, i, body, (tuple(m0), tuple(l0)))
    for c, hs in enumerate(heads):
        o_ref[:, hs] = (acc_ref[c] / ls[c]).T


def _moba_prompt(q_t, k_means, k_bf, v_t):
    w, t = q_t.shape
    nh = w // HEAD_DIM
    hp = math.gcd(nh, MOBA_HEADS)
    nb = t // MOBA_BLOCK
    assert t % MOBA_BLOCK == 0 and nb % SUBLANE == 0
    gw = hp * HEAD_DIM
    return pl.pallas_call(
        functools.partial(_moba_prompt_kernel, hp=hp),
        grid=(nh // hp, nb),
        in_specs=[pl.BlockSpec((gw, MOBA_BLOCK), lambda g, i: (g, i)),
                  pl.BlockSpec((nb, gw), lambda g, i: (0, g)),
                  pl.BlockSpec((t, gw), lambda g, i: (0, g)),
                  pl.BlockSpec((gw, t), lambda g, i: (g, 0))],
        out_specs=pl.BlockSpec((MOBA_BLOCK, gw), lambda g, i: (i, g)),
        out_shape=jax.ShapeDtypeStruct((t, w), F32),
        scratch_shapes=[pltpu.VMEM((hp, nb, MOBA_BLOCK), F32), pltpu.VMEM((hp, HEAD_DIM, MOBA_BLOCK), F32)],
        compiler_params=_params("parallel", "arbitrary"),
        name="moba_prompt",
    )(q_t, k_means, k_bf, v_t)


def _silu(x):
    return x * jax.nn.sigmoid(x)


def _softplus(x):
    return jnp.maximum(x, 0.0) + jnp.log1p(jnp.exp(-jnp.abs(x)))


def _conv_kernel(x_ref, halo_ref, prev_ref, w_ref, o_ref, *, nh):
    i = pl.program_id(0)
    j = pl.program_id(1)
    tm = x_ref.shape[0]
    base = SUBLANE - (CONV_WIDTH - 1)
    post = jnp.where(j == 0, HEAD_DIM ** -0.5, 1.0)
    for h in range(nh):
        cols = slice(h * HEAD_DIM, (h + 1) * HEAD_DIM)
        halo = jnp.where(i == 0, prev_ref[:, cols], halo_ref[:, cols])
        xp = jnp.concatenate([halo, x_ref[:, cols]], axis=0)
        w = w_ref[:, cols]
        conv = sum(xp[base + c:base + c + tm, :] * w[c:c + 1, :] for c in range(CONV_WIDTH))
        act = _silu(conv)
        inv = lax.rsqrt(jnp.sum(act * act, axis=-1, keepdims=True) + L2_EPS) * post
        o_ref[:, cols] = act * jnp.where(j == 2, 1.0, inv)


def _conv_prompt(raw, conv_prev, conv_w):
    t, cw = raw.shape
    sec = cw // 3
    nh = sec // HEAD_DIM
    tm = _tile(t, 256, SUBLANE)
    pad = SUBLANE - (CONV_WIDTH - 1)
    prev = jnp.pad(conv_prev.astype(F32), ((pad, 0), (0, 0)))
    wpad = jnp.pad(conv_w, ((0, SUBLANE - CONV_WIDTH), (0, 0)))
    rows_per = tm // SUBLANE
    return pl.pallas_call(
        functools.partial(_conv_kernel, nh=nh),
        grid=(t // tm, 3),
        in_specs=[
            pl.BlockSpec((tm, sec), lambda i, j: (i, j)),
            pl.BlockSpec((SUBLANE, sec), lambda i, j: (jnp.maximum(i * rows_per - 1, 0), j)),
            pl.BlockSpec((SUBLANE, sec), lambda i, j: (0, j)),
            pl.BlockSpec((SUBLANE, sec), lambda i, j: (0, j)),
        ],
        out_specs=pl.BlockSpec((tm, sec), lambda i, j: (i, j)),
        out_shape=jax.ShapeDtypeStruct((t, cw), F32),
        compiler_params=_params("parallel", "arbitrary"),
        name="conv_prompt",
    )(raw, raw, prev, wpad)


def _delta_prompt_kernel(q_ref, k_ref, v_ref, z_ref, ab_ref, alog_ref, dtb_ref, og_ref, o_ref, s_ref, *, hb):
    c = pl.program_id(1)
    n = DELTA_CHUNK

    @pl.when(c == 0)
    def _():
        s_ref[...] = jnp.zeros_like(s_ref)

    ab = ab_ref[...]
    g_all = -jnp.exp(alog_ref[...]) * _softplus(ab + dtb_ref[...])
    beta_all = jax.nn.sigmoid(ab)
    r_id = lax.broadcasted_iota(jnp.int32, (n, n), 0)
    c_id = lax.broadcasted_iota(jnp.int32, (n, n), 1)
    incl = r_id >= c_id
    strict = r_id > c_id
    eye = jnp.where(r_id == c_id, 1.0, 0.0)
    gcum = _dot(jnp.where(incl, 1.0, 0.0), g_all, HI)
    gcum_t = gcum.T
    og = og_ref[...]

    hs = range(hb)
    cols = [slice(h * HEAD_DIM, (h + 1) * HEAD_DIM) for h in hs]
    q = [q_ref[:, cols[h]] for h in hs]
    k = [k_ref[:, cols[h]] for h in hs]
    v = [v_ref[:, cols[h]] for h in hs]
    gc = [gcum[:, h:h + 1] for h in hs]
    bc = [beta_all[:, hb + h:hb + h + 1] for h in hs]
    eg = [jnp.exp(gc[h]) for h in hs]
    decay = [jnp.where(incl, jnp.exp(jnp.minimum(gc[h] - gcum_t[h:h + 1, :], 0.0)), 0.0) for h in hs]
    kk = _dot3_each(k, k, _dot_nt)
    low = [jnp.where(strict, kk[h] * bc[h] * decay[h], 0.0) for h in hs]
    inv = [eye - low[h] for h in hs]
    pw = low
    for _ in range(int(math.log2(n)) - 1):
        pw = _dot3_each(pw, pw)
        step = _dot3_each(pw, inv)
        inv = [inv[h] + step[h] for h in hs]
    uw = _dot3_each(inv, [jnp.concatenate([v[h] * bc[h], k[h] * (bc[h] * eg[h])], axis=1) for h in hs])
    s = [s_ref[h] for h in hs]
    ws = _dot3_each([uw[h][:, HEAD_DIM:] for h in hs], s)
    v_new = [uw[h][:, :HEAD_DIM] - ws[h] for h in hs]
    g_last = [gc[h][n - 1:n, :] for h in hs]
    kv = _dot3_each([k[h] * jnp.exp(g_last[h] - gc[h]) for h in hs], v_new, _dot_tn)
    for h in hs:
        s_ref[h] = s[h] * jnp.exp(g_last[h]) + kv[h]
    qk = [_dot_nt(q[h].astype(BF16), k[h].astype(BF16)) * decay[h] for h in hs]
    o_state = [_dot((q[h] * eg[h]).astype(BF16), s[h].astype(BF16)) for h in hs]
    o_chunk = [_dot(qk[h].astype(BF16), v_new[h].astype(BF16)) for h in hs]
    for h in hs:
        o_ref[:, cols[h]] = _rms(o_state[h] + o_chunk[h], og) * _silu(z_ref[:, cols[h]])


def _delta_prompt(act, z, ab, alog, dtb, o_gain, hb):
    t, cw = act.shape
    nh = cw // (3 * HEAD_DIM)
    ng = nh // hb
    assert t % DELTA_CHUNK == 0
    n = DELTA_CHUNK
    gw = hb * HEAD_DIM
    grp = lambda off: pl.BlockSpec((n, gw), lambda g, c: (c, g + off))
    vec = pl.BlockSpec((1, LANE), lambda g, c: (0, g))
    return pl.pallas_call(
        functools.partial(_delta_prompt_kernel, hb=hb),
        grid=(ng, t // n),
        in_specs=[grp(0), grp(ng), grp(2 * ng), grp(0),
                  pl.BlockSpec((n, LANE), lambda g, c: (c, g)), vec, vec,
                  pl.BlockSpec((1, HEAD_DIM), lambda g, c: (0, 0))],
        out_specs=[grp(0), pl.BlockSpec((hb, HEAD_DIM, HEAD_DIM), lambda g, c: (g, 0, 0))],
        out_shape=[jax.ShapeDtypeStruct((t, nh * HEAD_DIM), F32),
                   jax.ShapeDtypeStruct((nh, HEAD_DIM, HEAD_DIM), F32)],
        compiler_params=_params("parallel", "arbitrary"),
        name="delta_prompt",
    )(act, act, act, z, ab, alog, dtb, o_gain)


PAGE_BUFFERS = 8


def _page_gate_kernel(pt_ref, ck_hbm, q_ref, sel_ref, buf, sem, bm_ref, *, n_seq, n_pages):
    ppb = MOBA_BLOCK // PAGE_SIZE
    total = n_seq * n_pages

    def page_copy(t, slot):
        page = pt_ref[t // n_pages, t % n_pages]
        return pltpu.make_async_copy(ck_hbm.at[0, page], buf.at[slot], sem.at[slot])

    for t in range(min(PAGE_BUFFERS, total)):
        page_copy(t, t).start()

    def body(t, carry):
        slot = t % PAGE_BUFFERS
        b = t // n_pages
        p = t % n_pages
        blk = p // ppb
        page_copy(t, slot).wait()
        s = jnp.sum(buf[slot], axis=0)

        @pl.when(t + PAGE_BUFFERS < total)
        def _():
            page_copy(t + PAGE_BUFFERS, slot).start()

        @pl.when(p % ppb == 0)
        def _():
            bm_ref[blk] = s

        @pl.when(p % ppb != 0)
        def _():
            bm_ref[blk] = bm_ref[blk] + s

        @pl.when(p == n_pages - 1)
        def _():
            bm = bm_ref[...] * (1.0 / MOBA_BLOCK)
            gate = jnp.sum(bm * q_ref[b], axis=-1, keepdims=True)
            gate = jnp.broadcast_to(gate, bm.shape)
            idx = lax.broadcasted_iota(jnp.int32, gate.shape, 0)
            nb = gate.shape[0]
            for r in range(MOBA_TOPK):
                top = jnp.max(gate, axis=0, keepdims=True)
                first = jnp.min(jnp.where(gate == top, idx, nb), axis=0, keepdims=True)
                sel_ref[b, r] = first[0]
                gate = jnp.where(idx == first, -jnp.inf, gate)

        return carry

    lax.fori_loop(0, total, body, 0)


def _page_gate(page_table, cache_k, q):
    b, n_pages = page_table.shape
    _, _, page, nh, dh = cache_k.shape
    ppb = MOBA_BLOCK // PAGE_SIZE
    assert page == PAGE_SIZE and dh == HEAD_DIM and n_pages % ppb == 0 and n_pages // ppb >= MOBA_TOPK
    grid_spec = pltpu.PrefetchScalarGridSpec(
        num_scalar_prefetch=1,
        grid=(1,),
        in_specs=[
            pl.BlockSpec(memory_space=pl.ANY),
            pl.BlockSpec((b, nh, dh), lambda i, pt: (0, 0, 0)),
        ],
        out_specs=pl.BlockSpec((b, MOBA_TOPK, nh, dh), lambda i, pt: (0, 0, 0, 0)),
        scratch_shapes=[pltpu.VMEM((PAGE_BUFFERS, page, nh, dh), F32),
                        pltpu.SemaphoreType.DMA((PAGE_BUFFERS,)),
                        pltpu.VMEM((n_pages // ppb, nh, dh), F32)],
    )
    return pl.pallas_call(
        functools.partial(_page_gate_kernel, n_seq=b, n_pages=n_pages),
        grid_spec=grid_spec,
        out_shape=jax.ShapeDtypeStruct((b, MOBA_TOPK, nh, dh), jnp.int32),
        compiler_params=_params("arbitrary"),
        name="page_gate",
    )(page_table, cache_k, q)


def _paged_attn_kernel(pt_ref, sel_ref, ck_hbm, cv_hbm, q_ref, kn_ref, vn_ref, o_ref, kbuf, vbuf, sem,
                       *, n_seq, nh):
    ppb = MOBA_BLOCK // PAGE_SIZE
    n_slab = MOBA_TOPK * ppb
    total = n_seq * nh
    scale = HEAD_DIM ** -0.5

    def slab_copies(t, slot):
        b = t // nh
        h = t % nh
        out = []
        for s in range(MOBA_TOPK):
            blk = sel_ref[b, s * nh + h]
            for r in range(ppb):
                page = pt_ref[b, blk * ppb + r]
                i = s * ppb + r
                for c, (src, dst) in enumerate(((ck_hbm, kbuf), (cv_hbm, vbuf))):
                    out.append(pltpu.make_async_copy(
                        src.at[0, page, :, pl.ds(h, 1), :], dst.at[slot, i], sem.at[c, slot, i]))
        return out

    for cp in slab_copies(0, 0):
        cp.start()

    def body(t, carry):
        slot = t % 2
        b = t // nh
        h = t % nh

        @pl.when(t + 1 < total)
        def _():
            for cp in slab_copies(t + 1, 1 - slot):
                cp.start()

        for cp in slab_copies(t, slot):
            cp.wait()

        q = q_ref[b, pl.ds(h, 1), :]
        s_own = jnp.sum(q * kn_ref[b, pl.ds(h, 1), :], axis=-1, keepdims=True) * scale
        logits = [jnp.sum(kbuf[slot, i, :, 0, :] * q, axis=-1, keepdims=True) * scale for i in range(n_slab)]
        m = s_own
        for s in logits:
            m = jnp.maximum(m, jnp.max(s, axis=0, keepdims=True))
        p_own = jnp.exp(s_own - m)
        l = p_own
        acc = p_own * vn_ref[b, pl.ds(h, 1), :]
        for i, s in enumerate(logits):
            p = jnp.exp(s - m)
            l = l + jnp.sum(p, axis=0, keepdims=True)
            acc = acc + jnp.sum(p * vbuf[slot, i, :, 0, :], axis=0, keepdims=True)
        o_ref[b, pl.ds(h, 1), :] = acc / l
        return carry

    lax.fori_loop(0, total, body, 0)


def _paged_attn(page_table, sel, q, k_new, v_new, cache_k, cache_v):
    b, nh, dh = q.shape
    page = cache_k.shape[2]
    n_slab = MOBA_TOPK * (MOBA_BLOCK // PAGE_SIZE)
    full = pl.BlockSpec((b, nh, dh), lambda i, pt, sl: (0, 0, 0))
    hbm = pl.BlockSpec(memory_space=pl.ANY)
    grid_spec = pltpu.PrefetchScalarGridSpec(
        num_scalar_prefetch=2,
        grid=(1,),
        in_specs=[hbm, hbm, full, full, full],
        out_specs=full,
        scratch_shapes=[pltpu.VMEM((2, n_slab, page, 1, dh), F32), pltpu.VMEM((2, n_slab, page, 1, dh), F32),
                        pltpu.SemaphoreType.DMA((2, 2, n_slab))],
    )
    return pl.pallas_call(
        functools.partial(_paged_attn_kernel, n_seq=b, nh=nh),
        grid_spec=grid_spec,
        out_shape=jax.ShapeDtypeStruct((b, nh, dh), F32),
        compiler_params=_params("arbitrary"),
        name="paged_attn",
    )(page_table, sel, cache_k, cache_v, q, k_new, v_new)


def _delta_step_kernel(raw_ref, conv_ref, w_ref, z_ref, ab_ref, alog_ref, dtb_ref, og_ref, s_ref,
                       o_ref, convn_ref, sn_ref, *, nh):
    raw = raw_ref[0]
    prev = conv_ref[0]
    w = w_ref[...]
    conv = raw * w[CONV_WIDTH - 1]
    for c in range(CONV_WIDTH - 1):
        conv = conv + prev[c] * w[c]
    for c in range(CONV_WIDTH - 2):
        convn_ref[0, c] = prev[c + 1]
    convn_ref[0, CONV_WIDTH - 2] = raw
    act = _silu(conv)
    inv = lax.rsqrt(jnp.sum(act * act, axis=-1, keepdims=True) + L2_EPS)
    q = act[0:nh] * inv[0:nh] * (HEAD_DIM ** -0.5)
    k = act[nh:2 * nh] * inv[nh:2 * nh]
    v = act[2 * nh:3 * nh]
    ab = ab_ref[0]
    g = -jnp.exp(alog_ref[...]) * _softplus(ab + dtb_ref[...])
    beta = jax.nn.sigmoid(ab)
    pad = jnp.zeros((HEAD_DIM - nh, HEAD_DIM), F32)
    k_t = jnp.concatenate([k, pad], axis=0).T
    q_t = jnp.concatenate([q, pad], axis=0).T
    outs = []
    for h in range(nh):
        eg = jnp.exp(g[:, h:h + 1])
        bh = beta[:, nh + h:nh + h + 1]
        s = s_ref[0, h] * eg
        kc = k_t[:, h:h + 1]
        v_new = (v[h:h + 1, :] - jnp.sum(kc * s, axis=0, keepdims=True)) * bh
        s = s + kc * v_new
        sn_ref[0, h] = s
        outs.append(jnp.sum(q_t[:, h:h + 1] * s, axis=0, keepdims=True))
    o = jnp.concatenate(outs, axis=0)
    o_ref[0] = _rms(o, og_ref[...]) * _silu(z_ref[0])


def _delta_step(raw, state_conv, conv_w, z, ab, alog, dtb, o_gain, state):
    b, rows, dh = raw.shape
    nh = rows // 3
    assert 2 * nh <= LANE
    bs = lambda *shape: pl.BlockSpec((1,) + shape, lambda i: (i,) + (0,) * len(shape))
    const = lambda *shape: pl.BlockSpec(shape, lambda i: (0,) * len(shape))
    return pl.pallas_call(
        functools.partial(_delta_step_kernel, nh=nh),
        grid=(b,),
        in_specs=[bs(rows, dh), bs(CONV_WIDTH - 1, rows, dh), const(CONV_WIDTH, rows, dh), bs(nh, dh),
                  bs(1, LANE), const(1, LANE), const(1, LANE), const(1, dh), bs(nh, dh, dh)],
        out_specs=[bs(nh, dh), bs(CONV_WIDTH - 1, rows, dh), bs(nh, dh, dh)],
        out_shape=[jax.ShapeDtypeStruct((b, nh, dh), F32),
                   jax.ShapeDtypeStruct((b, CONV_WIDTH - 1, rows, dh), F32),
                   jax.ShapeDtypeStruct((b, nh, dh, dh), F32)],
        compiler_params=_params("parallel"),
        name="delta_step",
    )(raw, state_conv, conv_w, z, ab, alog, dtb, o_gain, state)


def _pad_lanes(x):
    return jnp.pad(x, ((0, 0), (0, LANE - x.shape[1])))


def kernel(x_prompt, x_sample, cache_k, cache_v, state_conv, state_delta, page_table, p_prompt, p_sample,
           ffn1_norm, ffn1_w_up, ffn1_w_down, mix_norm, w_in, q_norm, k_norm, conv_w, a_log, dt_bias,
           o_norm, w_out, ffn2_norm, ffn2_w_up, ffn2_w_down, ple_norm, ple_gate, ple_proj):
    depth = ffn1_norm.shape[0]
    assert depth == 1, "one trunk layer"
    bp, t, d = x_prompt.shape
    bs, ts, _ = x_sample.shape
    assert bp == 1 and ts == 1
    nh = d // HEAD_DIM
    hb = math.gcd(nh, 8)
    assert 2 * hb <= LANE
    ng = nh // hb
    past_len = page_table.shape[1] * PAGE_SIZE
    assert past_len % MOBA_BLOCK == 0

    w = w_in[0]
    o_qk, o_v, o_b, o_z, o_a, o_bb, o_g = 0, 2 * d, 3 * d, 6 * d, 7 * d, 7 * d + nh, 7 * d + 2 * nh
    w_qk = w[:, o_qk:o_v].astype(BF16)
    w_v = w[:, o_v:o_b].astype(BF16)
    w_b = w[:, o_b:o_z].astype(BF16)
    w_z = w[:, o_z:o_a].astype(BF16)
    w_gates = w[:, o_g:o_g + 2 * d].astype(BF16)
    w_a, w_bb = w[:, o_a:o_bb], w[:, o_bb:o_g]
    w_ab_s = _pad_lanes(jnp.concatenate([w_a, w_bb], axis=1)).astype(BF16)
    w_ab_p = jnp.concatenate(
        [_pad_lanes(jnp.concatenate([w_a[:, g * hb:(g + 1) * hb], w_bb[:, g * hb:(g + 1) * hb]], axis=1))
         for g in range(ng)], axis=1).astype(BF16)
    grp_vec = lambda x: jnp.concatenate([_pad_lanes(x[:, g * hb:(g + 1) * hb]) for g in range(ng)], axis=1)
    alog_p, dtb_p = grp_vec(a_log.astype(F32)), grp_vec(dt_bias.astype(F32))
    alog_s, dtb_s = _pad_lanes(a_log.astype(F32)), _pad_lanes(dt_bias.astype(F32))
    up1, down1 = ffn1_w_up[0].astype(BF16), ffn1_w_down[0].astype(BF16)
    up2, down2 = ffn2_w_up[0].astype(BF16), ffn2_w_down[0].astype(BF16)
    wo = w_out[0].astype(BF16)
    wpg, wpp = ple_gate[0].astype(BF16), ple_proj[0].astype(BF16)
    cw = conv_w[0].astype(F32)

    def trunk_in(x):
        h = _rmsnorm(x, ffn1_norm)
        x1, h1 = _ffn(x, h, up1, down1, mix_norm)
        return x1, h1

    def trunk_out(x1, o_a_, o_b_, gates, pe):
        x2, h2 = _outproj(x1, o_a_, o_b_, gates, wo, ffn2_norm)
        x3, h3 = _ffn(x2, h2, up2, down2, ple_norm)
        return _ple(x3, h3, pe.astype(BF16), wpg, wpp)

    xp = x_prompt[0]
    x1, h1 = trunk_in(xp)
    cos_p, sin_p = _rope_tables(jnp.arange(t, dtype=jnp.int32))
    qt_p, k_p, kb_p, km_p = _rope(_matmul(h1, w_qk), q_norm, k_norm, cos_p, sin_p, True)
    v_p = _matmul(h1, w_v)
    raw_p = _matmul(h1, w_b)
    z_p = _matmul(h1, w_z)
    ab_p = _matmul(h1, w_ab_p)
    gates_p = _matmul(h1, w_gates)
    oa_p = _moba_prompt(qt_p, km_p, kb_p, v_p.T.astype(BF16))
    conv0 = jnp.zeros((CONV_WIDTH - 1, 3 * d), F32)
    act_p = _conv_prompt(raw_p, conv0, cw)
    ob_p, delta_p = _delta_prompt(act_p, z_p, ab_p, alog_p, dtb_p, o_norm, hb)
    y_p = trunk_out(x1, oa_p, ob_p, gates_p, p_prompt[0, 0])
    conv_p = jnp.concatenate([conv0, raw_p], axis=0)[t:]

    xs = x_sample[:, 0]
    x1s, h1s = trunk_in(xs)
    pos_s = jnp.full((bs,), past_len, jnp.int32)
    cos_s, sin_s = _rope_tables(pos_s)
    q_s, k_s = _rope(_matmul(h1s, w_qk), q_norm, k_norm, cos_s, sin_s, False)
    v_s = _matmul(h1s, w_v)
    raw_s = _matmul(h1s, w_b)
    z_s = _matmul(h1s, w_z)
    ab_s = _matmul(h1s, w_ab_s)
    gates_s = _matmul(h1s, w_gates)
    by_head = lambda x: x.reshape(bs, nh, HEAD_DIM)
    sel = _page_gate(page_table, cache_k, by_head(q_s))
    sel = sel[:, :, :, 0].reshape(bs, MOBA_TOPK * nh)
    oa_s = _paged_attn(page_table, sel, by_head(q_s), by_head(k_s), by_head(v_s), cache_k, cache_v).reshape(bs, d)
    ob_s, conv_s, delta_s = _delta_step(
        raw_s.reshape(bs, 3 * nh, HEAD_DIM),
        state_conv[0].astype(F32).reshape(bs, CONV_WIDTH - 1, 3 * nh, HEAD_DIM),
        cw.reshape(CONV_WIDTH, 3 * nh, HEAD_DIM),
        z_s.reshape(bs, nh, HEAD_DIM), ab_s.reshape(bs, 1, LANE), alog_s, dtb_s, o_norm,
        state_delta[0].astype(F32))
    y_s = trunk_out(x1s, oa_s, ob_s.reshape(bs, d), gates_s, p_sample[0, :, 0])

    heads = lambda x, n: x.reshape(1, n, -1, nh, HEAD_DIM)
    return (y_p[None], y_s[:, None],
            heads(k_p, 1), heads(v_p, 1), conv_p[None, None], delta_p[None, None],
            heads(k_s, bs), heads(v_s, bs), conv_s.reshape(1, bs, CONV_WIDTH - 1, 3 * d), delta_s[None])
```

```python
import functools
import math

import jax
import jax.numpy as jnp
from jax import lax
from jax.experimental import pallas as pl
from jax.experimental.pallas import tpu as pltpu

F32 = jnp.float32
BF16 = jnp.bfloat16
HI = lax.Precision.HIGHEST

HEAD_DIM = 128
MOBA_BLOCK = 256
MOBA_TOPK = 3
PAGE_SIZE = 128
CONV_WIDTH = 4
DELTA_CHUNK = 64
ROPE_THETA = 10000.0
RMS_EPS = 1e-6
L2_EPS = 1e-6
NEG_INF = -1e30

LANE = 128
SUBLANE = 8
VMEM_LIMIT_BYTES = 56 * 1024 * 1024


def _params(*sem):
    return pltpu.CompilerParams(dimension_semantics=sem, vmem_limit_bytes=VMEM_LIMIT_BYTES)


def _tile(n, pref, quantum):
    if n <= pref:
        return n
    t = (pref // quantum) * quantum
    while t > quantum and n % t:
        t -= quantum
    assert n % t == 0, (n, pref, quantum)
    return t


def _rms(x, gain):
    return x * lax.rsqrt(jnp.mean(x * x, axis=-1, keepdims=True) + RMS_EPS) * gain


def _dot(a, b, precision=None):
    return jnp.dot(a, b, preferred_element_type=F32, precision=precision)


def _dot_nt(a, b, precision=None):
    return lax.dot_general(a, b, (((1,), (1,)), ((), ())), preferred_element_type=F32, precision=precision)


def _dot_tn(a, b, precision=None):
    return lax.dot_general(a, b, (((0,), (0,)), ((), ())), preferred_element_type=F32, precision=precision)


def _split(x):
    hi = x.astype(BF16)
    return hi, (x - hi.astype(F32)).astype(BF16)


def _dot3_each(a_list, b_list, dot=_dot):
    a_parts = [_split(a) for a in a_list]
    b_parts = [_split(b) for b in b_list]
    passes = [(dot(a_hi, b_hi), dot(a_hi, b_lo), dot(a_lo, b_hi))
              for (a_hi, a_lo), (b_hi, b_lo) in zip(a_parts, b_parts)]
    return [hh + hl + lh for hh, hl, lh in passes]


def _rmsnorm_kernel(x_ref, g_ref, o_ref):
    o_ref[...] = _rms(x_ref[...], g_ref[...]).astype(o_ref.dtype)


def _rmsnorm(x, gain):
    m, d = x.shape
    tm = _tile(m, 512, SUBLANE)
    return pl.pallas_call(
        _rmsnorm_kernel,
        grid=(m // tm,),
        in_specs=[pl.BlockSpec((tm, d), lambda i: (i, 0)), pl.BlockSpec((1, d), lambda i: (0, 0))],
        out_specs=pl.BlockSpec((tm, d), lambda i: (i, 0)),
        out_shape=jax.ShapeDtypeStruct((m, d), BF16),
        compiler_params=_params("parallel"),
        name="rmsnorm",
    )(x, gain)


FFN_PAGES = 6


def _ffn_kernel(*refs, nf, n_page):
    refs = refs[1:] if n_page else refs
    x_ref, h_ref, wg_ref, wu_ref, wd_ref, ng_ref = refs[:6]
    page_refs = refs[6:6 + n_page]
    y_ref, hn_ref = refs[6 + n_page:8 + n_page]
    acc_ref = refs[-1]
    f = pl.program_id(1)

    @pl.when(f == 0)
    def _():
        acc_ref[...] = jnp.zeros_like(acc_ref)

    h = h_ref[...]
    g = _dot(h, wg_ref[...])
    u = _dot(h, wu_ref[...])
    a = (g * jax.nn.sigmoid(g) * u).astype(BF16)
    acc_ref[...] += _dot(a, wd_ref[...])

    if n_page:
        psum_ref = refs[8 + n_page]
        for r, page_ref in enumerate(page_refs):
            psum_ref[0, r] = jnp.sum(page_ref[0, 0], axis=0)

    @pl.when(f == nf - 1)
    def _():
        y = x_ref[...] + 0.5 * acc_ref[...]
        y_ref[...] = y
        hn_ref[...] = _rms(y, ng_ref[...]).astype(hn_ref.dtype)


def _ffn_steps(m, ff, with_pages):
    tm = _tile(m, 512, SUBLANE)
    tf = _tile(ff, 256 if with_pages else 512, LANE)
    return tm, tf, (m // tm) * (ff // tf)


def _ffn(x, h, w_up, w_down, next_gain, pages=None):
    m, d = x.shape
    ff = w_down.shape[0]
    tm, tf, steps = _ffn_steps(m, ff, pages is not None)
    nf = ff // tf
    n_page = FFN_PAGES if pages is not None else 0
    in_specs = [
        pl.BlockSpec((tm, d), lambda i, f, *_: (i, 0)),
        pl.BlockSpec((tm, d), lambda i, f, *_: (i, 0)),
        pl.BlockSpec((d, tf), lambda i, f, *_: (0, f)),
        pl.BlockSpec((d, tf), lambda i, f, *_: (0, f + nf)),
        pl.BlockSpec((tf, d), lambda i, f, *_: (f, 0)),
        pl.BlockSpec((1, d), lambda i, f, *_: (0, 0)),
    ]
    out_specs = [pl.BlockSpec((tm, d), lambda i, f, *_: (i, 0)), pl.BlockSpec((tm, d), lambda i, f, *_: (i, 0))]
    out_shape = [jax.ShapeDtypeStruct((m, d), F32), jax.ShapeDtypeStruct((m, d), BF16)]
    args = [x, h, w_up, w_up, w_down, next_gain]
    prefetch = []
    if pages is not None:
        page_ids, cache, first = pages
        _, _, page, nh, dh = cache.shape
        last = page_ids.shape[0] - 1

        def page_spec(r):
            def index(i, f, ids):
                return (0, ids[jnp.minimum(first + (i * nf + f) * n_page + r, last)], 0, 0, 0)
            return pl.BlockSpec((1, 1, page, nh, dh), index)

        in_specs += [page_spec(r) for r in range(n_page)]
        out_specs.append(pl.BlockSpec((1, n_page, nh, dh), lambda i, f, *_: (i * nf + f, 0, 0, 0)))
        out_shape.append(jax.ShapeDtypeStruct((steps, n_page, nh, dh), F32))
        args += [cache] * n_page
        prefetch = [page_ids]
    outs = pl.pallas_call(
        functools.partial(_ffn_kernel, nf=nf, n_page=n_page),
        grid_spec=pltpu.PrefetchScalarGridSpec(
            num_scalar_prefetch=len(prefetch),
            grid=(m // tm, nf),
            in_specs=in_specs,
            out_specs=out_specs,
            scratch_shapes=[pltpu.VMEM((tm, d), F32)],
        ),
        out_shape=out_shape,
        compiler_params=_params("parallel", "arbitrary"),
        name="ffn",
    )(*prefetch, *args)
    if pages is None:
        return outs
    y, hn, psum = outs
    return y, hn, psum.reshape(steps * n_page, nh, dh)


def _mm_kernel(h_ref, w_ref, o_ref, *maybe_ot_ref):
    o = _dot(h_ref[...], w_ref[...])
    o_ref[...] = o
    for ot_ref in maybe_ot_ref:
        ot_ref[...] = o.T.astype(ot_ref.dtype)


def _matmul(h, w, transposed_copy=False):
    m, kd = h.shape
    n = w.shape[1]
    tm = _tile(m, 512, LANE if transposed_copy else SUBLANE)
    tn = _tile(n, 1024, LANE)
    out_specs = [pl.BlockSpec((tm, tn), lambda i, j: (i, j))]
    out_shape = [jax.ShapeDtypeStruct((m, n), F32)]
    if transposed_copy:
        out_specs.append(pl.BlockSpec((tn, tm), lambda i, j: (j, i)))
        out_shape.append(jax.ShapeDtypeStruct((n, m), BF16))
    outs = pl.pallas_call(
        _mm_kernel,
        grid=(m // tm, n // tn),
        in_specs=[pl.BlockSpec((tm, kd), lambda i, j: (i, 0)), pl.BlockSpec((kd, tn), lambda i, j: (0, j))],
        out_specs=out_specs,
        out_shape=out_shape,
        compiler_params=_params("parallel", "arbitrary"),
        name="proj",
    )(h, w)
    return outs if transposed_copy else outs[0]


def _outproj_kernel(x_ref, oa_ref, ob_ref, ga_ref, gb_ref, w_ref, ng_ref, y_ref, hn_ref):
    merged = jax.nn.sigmoid(ga_ref[...]) * oa_ref[...] + jax.nn.sigmoid(gb_ref[...]) * ob_ref[...]
    y = x_ref[...] + _dot(merged.astype(BF16), w_ref[...])
    y_ref[...] = y
    hn_ref[...] = _rms(y, ng_ref[...]).astype(hn_ref.dtype)


def _outproj(x, o_a, o_b, gates, w_out, next_gain):
    m, d = x.shape
    tm = _tile(m, 256, SUBLANE)
    row = lambda i: (i, 0)
    return pl.pallas_call(
        _outproj_kernel,
        grid=(m // tm,),
        in_specs=[
            pl.BlockSpec((tm, d), row),
            pl.BlockSpec((tm, d), row),
            pl.BlockSpec((tm, d), row),
            pl.BlockSpec((tm, d), lambda i: (i, 0)),
            pl.BlockSpec((tm, d), lambda i: (i, 1)),
            pl.BlockSpec((d, d), lambda i: (0, 0)),
            pl.BlockSpec((1, d), lambda i: (0, 0)),
        ],
        out_specs=[pl.BlockSpec((tm, d), row), pl.BlockSpec((tm, d), row)],
        out_shape=[jax.ShapeDtypeStruct((m, d), F32), jax.ShapeDtypeStruct((m, d), BF16)],
        compiler_params=_params("parallel"),
        name="outproj",
    )(x, o_a, o_b, gates, gates, w_out, next_gain)


def _ple_kernel(x_ref, h_ref, pe_ref, wg_ref, wp_ref, y_ref):
    gate = jax.nn.sigmoid(_dot(h_ref[...], wg_ref[...]))
    y_ref[...] = x_ref[...] + gate * _dot(pe_ref[...], wp_ref[...])


def _ple(x, h, pe, w_gate, w_proj):
    m, d = x.shape
    pd = pe.shape[1]
    tm = _tile(m, 512, SUBLANE)
    tn = _tile(d, 1024, LANE)
    return pl.pallas_call(
        _ple_kernel,
        grid=(m // tm, d // tn),
        in_specs=[
            pl.BlockSpec((tm, tn), lambda i, j: (i, j)),
            pl.BlockSpec((tm, d), lambda i, j: (i, 0)),
            pl.BlockSpec((tm, pd), lambda i, j: (i, 0)),
            pl.BlockSpec((d, tn), lambda i, j: (0, j)),
            pl.BlockSpec((pd, tn), lambda i, j: (0, j)),
        ],
        out_specs=pl.BlockSpec((tm, tn), lambda i, j: (i, j)),
        out_shape=jax.ShapeDtypeStruct((m, d), F32),
        compiler_params=_params("parallel", "arbitrary"),
        name="ple",
    )(x, h, pe, w_gate, w_proj)


def _rope_kernel(xq_ref, xk_ref, gq_ref, gk_ref, c_ref, s_ref, q_ref, k_ref, *prompt_refs, for_prompt):
    c = c_ref[...]
    s = s_ref[...]

    def norm_rot(x, gain):
        y = _rms(x, gain)
        return y * c + pltpu.roll(y, HEAD_DIM // 2, 1) * s

    q = norm_rot(xq_ref[...], gq_ref[...])
    k = norm_rot(xk_ref[...], gk_ref[...])
    k_ref[...] = k
    if for_prompt:
        kb_ref, km_ref = prompt_refs
        q_ref[...] = q.T
        kb_ref[...] = k.astype(kb_ref.dtype)
        km_ref[...] = jnp.mean(k.reshape(km_ref.shape[0], MOBA_BLOCK, HEAD_DIM), axis=1)
    else:
        q_ref[...] = q


def _rope(qk_raw, q_gain, k_gain, cos_tab, sin_tab, for_prompt):
    m, w2 = qk_raw.shape
    nh = w2 // (2 * HEAD_DIM)
    tm = _tile(m, SUBLANE * MOBA_BLOCK, SUBLANE * MOBA_BLOCK) if for_prompt else _tile(m, 1024, SUBLANE)
    blk = lambda off: pl.BlockSpec((tm, HEAD_DIM), lambda i, h: (i, h + off))
    vec = pl.BlockSpec((1, HEAD_DIM), lambda i, h: (0, 0))
    tab = pl.BlockSpec((tm, HEAD_DIM), lambda i, h: (i, 0))
    out = jax.ShapeDtypeStruct((m, nh * HEAD_DIM), F32)
    if for_prompt:
        assert m % tm == 0 and tm % (SUBLANE * MOBA_BLOCK) == 0
        out_specs = [pl.BlockSpec((HEAD_DIM, tm), lambda i, h: (h, i)), blk(0), blk(0),
                     pl.BlockSpec((tm // MOBA_BLOCK, HEAD_DIM), lambda i, h: (i, h))]
        out_shape = [jax.ShapeDtypeStruct((nh * HEAD_DIM, m), F32), out,
                     jax.ShapeDtypeStruct((m, nh * HEAD_DIM), BF16),
                     jax.ShapeDtypeStruct((m // MOBA_BLOCK, nh * HEAD_DIM), F32)]
    else:
        out_specs, out_shape = [blk(0), blk(0)], [out, out]
    return pl.pallas_call(
        functools.partial(_rope_kernel, for_prompt=for_prompt),
        grid=(m // tm, nh),
        in_specs=[blk(0), blk(nh), vec, vec, tab, tab],
        out_specs=out_specs,
        out_shape=out_shape,
        compiler_params=_params("parallel", "arbitrary"),
        name="rope",
    )(qk_raw, qk_raw, q_gain, k_gain, cos_tab, sin_tab)


def _rope_tables(pos):
    half = HEAD_DIM // 2
    inv_freq = ROPE_THETA ** (-jnp.arange(half, dtype=F32) / half)
    ang = pos.astype(F32)[:, None] * inv_freq[None, :]
    cos, sin = jnp.cos(ang), jnp.sin(ang)
    return jnp.concatenate([cos, cos], axis=1), jnp.concatenate([-sin, sin], axis=1)


def _topk_mask(gate, valid, axis):
    n = gate.shape[axis]
    idx = lax.broadcasted_iota(jnp.int32, gate.shape, axis)
    g = jnp.where(valid, gate, NEG_INF)
    sel = jnp.zeros(gate.shape, F32)
    for _ in range(MOBA_TOPK):
        top = jnp.max(g, axis=axis, keepdims=True)
        first = jnp.min(jnp.where(g == top, idx, n), axis=axis, keepdims=True)
        hit = idx == first
        sel = jnp.where(hit, jnp.where(valid, 1.0, 0.0), sel)
        g = jnp.where(hit, -jnp.inf, g)
    return sel


MOBA_HEADS = 4


def _moba_prompt_kernel(qt_ref, km_ref, kb_ref, vt_ref, o_ref, sel_ref, acc_ref, *, hp):
    i = pl.program_id(1)
    blk = MOBA_BLOCK
    q_scale = HEAD_DIM ** -0.5 * math.log2(math.e)
    heads = [slice(c * HEAD_DIM, (c + 1) * HEAD_DIM) for c in range(hp)]
    row0 = pl.multiple_of(i * blk, blk)
    k_id = lax.broadcasted_iota(jnp.int32, (blk, blk), 0)
    q_id = lax.broadcasted_iota(jnp.int32, (blk, blk), 1)

    qs, m0, l0 = [], [], []
    for c, hs in enumerate(heads):
        qt = qt_ref[hs, :]
        gate = _dot(km_ref[:, hs], qt, HI)
        bid = lax.broadcasted_iota(jnp.int32, gate.shape, 0)
        sel_ref[c] = _topk_mask(gate, bid < i, 0)
        q_c = (qt * q_scale).astype(BF16)
        s = jnp.where(k_id <= q_id, _dot(kb_ref[pl.ds(row0, blk), hs], q_c), NEG_INF)
        m_c = jnp.max(s, axis=0, keepdims=True)
        p = jnp.exp2(s - m_c)
        acc_ref[c] = _dot(vt_ref[hs, pl.ds(row0, blk)], p.astype(BF16))
        qs.append(q_c)
        m0.append(m_c)
        l0.append(jnp.sum(p, axis=0, keepdims=True))

    def body(j, carry):
        ms, ls = carry
        r0 = pl.multiple_of(j * blk, blk)
        scores = [_dot(kb_ref[pl.ds(r0, blk), hs], qs[c]) for c, hs in enumerate(heads)]
        new_m, new_l, alphas, pvs = [], [], [], []
        for c, hs in enumerate(heads):
            pick = sel_ref[c, pl.ds(j, 1), :]
            s = jnp.where(pick > 0.0, scores[c], NEG_INF)
            m_new = jnp.maximum(ms[c], jnp.max(s, axis=0, keepdims=True))
            p = jnp.exp2(s - m_new)
            alpha = jnp.exp2(ms[c] - m_new)
            new_l.append(alpha * ls[c] + jnp.sum(p, axis=0, keepdims=True))
            pvs.append(_dot(vt_ref[hs, pl.ds(r0, blk)], p.astype(BF16)))
            alphas.append(alpha)
            new_m.append(m_new)
        for c in range(hp):
            acc_ref[c] = alphas[c] * acc_ref[c] + pvs[c]
        return tuple(new_m), tuple(new_l)

    _, ls = lax.fori_loop(0, i, body, (tuple(m0), tuple(l0)))
    for c, hs in enumerate(heads):
        o_ref[:, hs] = (acc_ref[c] / ls[c]).T


def _moba_prompt(q_t, k_means, k_bf, v_t):
    w, t = q_t.shape
    nh = w // HEAD_DIM
    hp = math.gcd(nh, MOBA_HEADS)
    nb = t // MOBA_BLOCK
    assert t % MOBA_BLOCK == 0 and nb % SUBLANE == 0
    gw = hp * HEAD_DIM
    return pl.pallas_call(
        functools.partial(_moba_prompt_kernel, hp=hp),
        grid=(nh // hp, nb),
        in_specs=[pl.BlockSpec((gw, MOBA_BLOCK), lambda g, i: (g, i)),
                  pl.BlockSpec((nb, gw), lambda g, i: (0, g)),
                  pl.BlockSpec((t, gw), lambda g, i: (0, g)),
                  pl.BlockSpec((gw, t), lambda g, i: (g, 0))],
        out_specs=pl.BlockSpec((MOBA_BLOCK, gw), lambda g, i: (i, g)),
        out_shape=jax.ShapeDtypeStruct((t, w), F32),
        scratch_shapes=[pltpu.VMEM((hp, nb, MOBA_BLOCK), F32), pltpu.VMEM((hp, HEAD_DIM, MOBA_BLOCK), F32)],
        compiler_params=_params("parallel", "arbitrary"),
        name="moba_prompt",
    )(q_t, k_means, k_bf, v_t)


def _silu(x):
    return x * jax.nn.sigmoid(x)


def _softplus(x):
    return jnp.maximum(x, 0.0) + jnp.log1p(jnp.exp(-jnp.abs(x)))


def _conv_kernel(x_ref, halo_ref, prev_ref, w_ref, o_ref, *, nh):
    i = pl.program_id(0)
    j = pl.program_id(1)
    tm = x_ref.shape[0]
    base = SUBLANE - (CONV_WIDTH - 1)
    post = jnp.where(j == 0, HEAD_DIM ** -0.5, 1.0)
    for h in range(nh):
        cols = slice(h * HEAD_DIM, (h + 1) * HEAD_DIM)
        halo = jnp.where(i == 0, prev_ref[:, cols], halo_ref[:, cols])
        xp = jnp.concatenate([halo, x_ref[:, cols]], axis=0)
        w = w_ref[:, cols]
        conv = sum(xp[base + c:base + c + tm, :] * w[c:c + 1, :] for c in range(CONV_WIDTH))
        act = _silu(conv)
        inv = lax.rsqrt(jnp.sum(act * act, axis=-1, keepdims=True) + L2_EPS) * post
        o_ref[:, cols] = act * jnp.where(j == 2, 1.0, inv)


def _conv_prompt(raw, conv_prev, conv_w):
    t, cw = raw.shape
    sec = cw // 3
    nh = sec // HEAD_DIM
    tm = _tile(t, 256, SUBLANE)
    pad = SUBLANE - (CONV_WIDTH - 1)
    prev = jnp.pad(conv_prev.astype(F32), ((pad, 0), (0, 0)))
    wpad = jnp.pad(conv_w, ((0, SUBLANE - CONV_WIDTH), (0, 0)))
    rows_per = tm // SUBLANE
    return pl.pallas_call(
        functools.partial(_conv_kernel, nh=nh),
        grid=(t // tm, 3),
        in_specs=[
            pl.BlockSpec((tm, sec), lambda i, j: (i, j)),
            pl.BlockSpec((SUBLANE, sec), lambda i, j: (jnp.maximum(i * rows_per - 1, 0), j)),
            pl.BlockSpec((SUBLANE, sec), lambda i, j: (0, j)),
            pl.BlockSpec((SUBLANE, sec), lambda i, j: (0, j)),
        ],
        out_specs=pl.BlockSpec((tm, sec), lambda i, j: (i, j)),
        out_shape=jax.ShapeDtypeStruct((t, cw), F32),
        compiler_params=_params("parallel", "arbitrary"),
        name="conv_prompt",
    )(raw, raw, prev, wpad)


def _delta_prompt_kernel(q_ref, k_ref, v_ref, z_ref, ab_ref, alog_ref, dtb_ref, og_ref, o_ref, s_ref, *, hb):
    c = pl.program_id(1)
    n = DELTA_CHUNK

    @pl.when(c == 0)
    def _():
        s_ref[...] = jnp.zeros_like(s_ref)

    ab = ab_ref[...]
    g_all = -jnp.exp(alog_ref[...]) * _softplus(ab + dtb_ref[...])
    beta_all = jax.nn.sigmoid(ab)
    r_id = lax.broadcasted_iota(jnp.int32, (n, n), 0)
    c_id = lax.broadcasted_iota(jnp.int32, (n, n), 1)
    incl = r_id >= c_id
    strict = r_id > c_id
    eye = jnp.where(r_id == c_id, 1.0, 0.0)
    gcum = _dot(jnp.where(incl, 1.0, 0.0), g_all, HI)
    gcum_t = gcum.T
    og = og_ref[...]

    hs = range(hb)
    cols = [slice(h * HEAD_DIM, (h + 1) * HEAD_DIM) for h in hs]
    q = [q_ref[:, cols[h]] for h in hs]
    k = [k_ref[:, cols[h]] for h in hs]
    v = [v_ref[:, cols[h]] for h in hs]
    gc = [gcum[:, h:h + 1] for h in hs]
    bc = [beta_all[:, hb + h:hb + h + 1] for h in hs]
    eg = [jnp.exp(gc[h]) for h in hs]
    decay = [jnp.where(incl, jnp.exp(jnp.minimum(gc[h] - gcum_t[h:h + 1, :], 0.0)), 0.0) for h in hs]
    kk = _dot3_each(k, k, _dot_nt)
    low = [jnp.where(strict, kk[h] * bc[h] * decay[h], 0.0) for h in hs]
    inv = [eye - low[h] for h in hs]
    pw = low
    for _ in range(int(math.log2(n)) - 1):
        pw = _dot3_each(pw, pw)
        step = _dot3_each(pw, inv)
        inv = [inv[h] + step[h] for h in hs]
    uw = _dot3_each(inv, [jnp.concatenate([v[h] * bc[h], k[h] * (bc[h] * eg[h])], axis=1) for h in hs])
    s = [s_ref[h] for h in hs]
    ws = _dot3_each([uw[h][:, HEAD_DIM:] for h in hs], s)
    v_new = [uw[h][:, :HEAD_DIM] - ws[h] for h in hs]
    g_last = [gc[h][n - 1:n, :] for h in hs]
    kv = _dot3_each([k[h] * jnp.exp(g_last[h] - gc[h]) for h in hs], v_new, _dot_tn)
    for h in hs:
        s_ref[h] = s[h] * jnp.exp(g_last[h]) + kv[h]
    qk = [_dot_nt(q[h].astype(BF16), k[h].astype(BF16)) * decay[h] for h in hs]
    o_state = [_dot((q[h] * eg[h]).astype(BF16), s[h].astype(BF16)) for h in hs]
    o_chunk = [_dot(qk[h].astype(BF16), v_new[h].astype(BF16)) for h in hs]
    for h in hs:
        o_ref[:, cols[h]] = _rms(o_state[h] + o_chunk[h], og) * _silu(z_ref[:, cols[h]])


def _delta_prompt(act, z, ab, alog, dtb, o_gain, hb):
    t, cw = act.shape
    nh = cw // (3 * HEAD_DIM)
    ng = nh // hb
    assert t % DELTA_CHUNK == 0
    n = DELTA_CHUNK
    gw = hb * HEAD_DIM
    grp = lambda off: pl.BlockSpec((n, gw), lambda g, c: (c, g + off))
    vec = pl.BlockSpec((1, LANE), lambda g, c: (0, g))
    return pl.pallas_call(
        functools.partial(_delta_prompt_kernel, hb=hb),
        grid=(ng, t // n),
        in_specs=[grp(0), grp(ng), grp(2 * ng), grp(0),
                  pl.BlockSpec((n, LANE), lambda g, c: (c, g)), vec, vec,
                  pl.BlockSpec((1, HEAD_DIM), lambda g, c: (0, 0))],
        out_specs=[grp(0), pl.BlockSpec((hb, HEAD_DIM, HEAD_DIM), lambda g, c: (g, 0, 0))],
        out_shape=[jax.ShapeDtypeStruct((t, nh * HEAD_DIM), F32),
                   jax.ShapeDtypeStruct((nh, HEAD_DIM, HEAD_DIM), F32)],
        compiler_params=_params("parallel", "arbitrary"),
        name="delta_prompt",
    )(act, act, act, z, ab, alog, dtb, o_gain)


def _block_gate_kernel(ps_ref, q_ref, sel_ref):
    ppb = MOBA_BLOCK // PAGE_SIZE
    ps = ps_ref[0]
    nb = ps.shape[0] // ppb
    bm = jnp.sum(ps.reshape(nb, ppb, *ps.shape[1:]), axis=1) * (1.0 / MOBA_BLOCK)
    gate = jnp.sum(bm * q_ref[...], axis=-1, keepdims=True)
    gate = jnp.broadcast_to(gate, bm.shape)
    idx = lax.broadcasted_iota(jnp.int32, gate.shape, 0)
    for r in range(MOBA_TOPK):
        top = jnp.max(gate, axis=0, keepdims=True)
        first = jnp.min(jnp.where(gate == top, idx, nb), axis=0, keepdims=True)
        sel_ref[0, r] = first[0]
        gate = jnp.where(idx == first, -jnp.inf, gate)


def _block_gate(page_sums, q):
    b, n_pages, nh, dh = page_sums.shape
    ppb = MOBA_BLOCK // PAGE_SIZE
    assert dh == HEAD_DIM and n_pages % ppb == 0 and n_pages // ppb >= MOBA_TOPK
    return pl.pallas_call(
        _block_gate_kernel,
        grid=(b,),
        in_specs=[pl.BlockSpec((1, n_pages, nh, dh), lambda i: (i, 0, 0, 0)),
                  pl.BlockSpec((1, nh, dh), lambda i: (i, 0, 0))],
        out_specs=pl.BlockSpec((1, MOBA_TOPK, nh, dh), lambda i: (i, 0, 0, 0)),
        out_shape=jax.ShapeDtypeStruct((b, MOBA_TOPK, nh, dh), jnp.int32),
        compiler_params=_params("parallel"),
        name="block_gate",
    )(page_sums, q)


def _paged_attn_kernel(pt_ref, sel_ref, ck_hbm, cv_hbm, q_ref, kn_ref, vn_ref, o_ref, kbuf, vbuf, sem,
                       *, n_seq, nh):
    ppb = MOBA_BLOCK // PAGE_SIZE
    n_slab = MOBA_TOPK * ppb
    total = n_seq * nh
    scale = HEAD_DIM ** -0.5

    def slab_copies(t, slot):
        b = t // nh
        h = t % nh
        out = []
        for s in range(MOBA_TOPK):
            blk = sel_ref[b, s * nh + h]
            for r in range(ppb):
                page = pt_ref[b, blk * ppb + r]
                i = s * ppb + r
                for c, (src, dst) in enumerate(((ck_hbm, kbuf), (cv_hbm, vbuf))):
                    out.append(pltpu.make_async_copy(
                        src.at[0, page, :, pl.ds(h, 1), :], dst.at[slot, i], sem.at[c, slot, i]))
        return out

    for cp in slab_copies(0, 0):
        cp.start()

    def body(t, carry):
        slot = t % 2
        b = t // nh
        h = t % nh

        @pl.when(t + 1 < total)
        def _():
            for cp in slab_copies(t + 1, 1 - slot):
                cp.start()

        for cp in slab_copies(t, slot):
            cp.wait()

        q = q_ref[b, pl.ds(h, 1), :]
        s_own = jnp.sum(q * kn_ref[b, pl.ds(h, 1), :], axis=-1, keepdims=True) * scale
        logits = [jnp.sum(kbuf[slot, i, :, 0, :] * q, axis=-1, keepdims=True) * scale for i in range(n_slab)]
        m = s_own
        for s in logits:
            m = jnp.maximum(m, jnp.max(s, axis=0, keepdims=True))
        p_own = jnp.exp(s_own - m)
        l = p_own
        acc = p_own * vn_ref[b, pl.ds(h, 1), :]
        for i, s in enumerate(logits):
            p = jnp.exp(s - m)
            l = l + jnp.sum(p, axis=0, keepdims=True)
            acc = acc + jnp.sum(p * vbuf[slot, i, :, 0, :], axis=0, keepdims=True)
        o_ref[b, pl.ds(h, 1), :] = acc / l
        return carry

    lax.fori_loop(0, total, body, 0)


def _paged_attn(page_table, sel, q, k_new, v_new, cache_k, cache_v):
    b, nh, dh = q.shape
    page = cache_k.shape[2]
    n_slab = MOBA_TOPK * (MOBA_BLOCK // PAGE_SIZE)
    full = pl.BlockSpec((b, nh, dh), lambda i, pt, sl: (0, 0, 0))
    hbm = pl.BlockSpec(memory_space=pl.ANY)
    grid_spec = pltpu.PrefetchScalarGridSpec(
        num_scalar_prefetch=2,
        grid=(1,),
        in_specs=[hbm, hbm, full, full, full],
        out_specs=full,
        scratch_shapes=[pltpu.VMEM((2, n_slab, page, 1, dh), F32), pltpu.VMEM((2, n_slab, page, 1, dh), F32),
                        pltpu.SemaphoreType.DMA((2, 2, n_slab))],
    )
    return pl.pallas_call(
        functools.partial(_paged_attn_kernel, n_seq=b, nh=nh),
        grid_spec=grid_spec,
        out_shape=jax.ShapeDtypeStruct((b, nh, dh), F32),
        compiler_params=_params("arbitrary"),
        name="paged_attn",
    )(page_table, sel, cache_k, cache_v, q, k_new, v_new)


def _delta_step_kernel(raw_ref, conv_ref, w_ref, z_ref, ab_ref, alog_ref, dtb_ref, og_ref, s_ref,
                       o_ref, convn_ref, sn_ref, *, nh):
    raw = raw_ref[0]
    prev = conv_ref[0]
    w = w_ref[...]
    conv = raw * w[CONV_WIDTH - 1]
    for c in range(CONV_WIDTH - 1):
        conv = conv + prev[c] * w[c]
    for c in range(CONV_WIDTH - 2):
        convn_ref[0, c] = prev[c + 1]
    convn_ref[0, CONV_WIDTH - 2] = raw
    act = _silu(conv)
    inv = lax.rsqrt(jnp.sum(act * act, axis=-1, keepdims=True) + L2_EPS)
    q = act[0:nh] * inv[0:nh] * (HEAD_DIM ** -0.5)
    k = act[nh:2 * nh] * inv[nh:2 * nh]
    v = act[2 * nh:3 * nh]
    ab = ab_ref[0]
    g = -jnp.exp(alog_ref[...]) * _softplus(ab + dtb_ref[...])
    beta = jax.nn.sigmoid(ab)
    pad = jnp.zeros((HEAD_DIM - nh, HEAD_DIM), F32)
    k_t = jnp.concatenate([k, pad], axis=0).T
    q_t = jnp.concatenate([q, pad], axis=0).T
    outs = []
    for h in range(nh):
        eg = jnp.exp(g[:, h:h + 1])
        bh = beta[:, nh + h:nh + h + 1]
        s = s_ref[0, h] * eg
        kc = k_t[:, h:h + 1]
        v_new = (v[h:h + 1, :] - jnp.sum(kc * s, axis=0, keepdims=True)) * bh
        s = s + kc * v_new
        sn_ref[0, h] = s
        outs.append(jnp.sum(q_t[:, h:h + 1] * s, axis=0, keepdims=True))
    o = jnp.concatenate(outs, axis=0)
    o_ref[0] = _rms(o, og_ref[...]) * _silu(z_ref[0])


def _delta_step(raw, state_conv, conv_w, z, ab, alog, dtb, o_gain, state):
    b, rows, dh = raw.shape
    nh = rows // 3
    assert 2 * nh <= LANE
    bs = lambda *shape: pl.BlockSpec((1,) + shape, lambda i: (i,) + (0,) * len(shape))
    const = lambda *shape: pl.BlockSpec(shape, lambda i: (0,) * len(shape))
    return pl.pallas_call(
        functools.partial(_delta_step_kernel, nh=nh),
        grid=(b,),
        in_specs=[bs(rows, dh), bs(CONV_WIDTH - 1, rows, dh), const(CONV_WIDTH, rows, dh), bs(nh, dh),
                  bs(1, LANE), const(1, LANE), const(1, LANE), const(1, dh), bs(nh, dh, dh)],
        out_specs=[bs(nh, dh), bs(CONV_WIDTH - 1, rows, dh), bs(nh, dh, dh)],
        out_shape=[jax.ShapeDtypeStruct((b, nh, dh), F32),
                   jax.ShapeDtypeStruct((b, CONV_WIDTH - 1, rows, dh), F32),
                   jax.ShapeDtypeStruct((b, nh, dh, dh), F32)],
        compiler_params=_params("parallel"),
        name="delta_step",
    )(raw, state_conv, conv_w, z, ab, alog, dtb, o_gain, state)


def _pad_lanes(x):
    return jnp.pad(x, ((0, 0), (0, LANE - x.shape[1])))


def kernel(x_prompt, x_sample, cache_k, cache_v, state_conv, state_delta, page_table, p_prompt, p_sample,
           ffn1_norm, ffn1_w_up, ffn1_w_down, mix_norm, w_in, q_norm, k_norm, conv_w, a_log, dt_bias,
           o_norm, w_out, ffn2_norm, ffn2_w_up, ffn2_w_down, ple_norm, ple_gate, ple_proj):
    depth = ffn1_norm.shape[0]
    assert depth == 1, "one trunk layer"
    bp, t, d = x_prompt.shape
    bs, ts, _ = x_sample.shape
    assert bp == 1 and ts == 1
    nh = d // HEAD_DIM
    hb = math.gcd(nh, 16)
    assert 2 * hb <= LANE
    ng = nh // hb
    past_len = page_table.shape[1] * PAGE_SIZE
    assert past_len % MOBA_BLOCK == 0

    w = w_in[0]
    o_qk, o_v, o_b, o_z, o_a, o_bb, o_g = 0, 2 * d, 3 * d, 6 * d, 7 * d, 7 * d + nh, 7 * d + 2 * nh
    w_qk = w[:, o_qk:o_v].astype(BF16)
    w_v = w[:, o_v:o_b].astype(BF16)
    w_b = w[:, o_b:o_z].astype(BF16)
    w_z = w[:, o_z:o_a].astype(BF16)
    w_gates = w[:, o_g:o_g + 2 * d].astype(BF16)
    w_a, w_bb = w[:, o_a:o_bb], w[:, o_bb:o_g]
    w_ab_s = _pad_lanes(jnp.concatenate([w_a, w_bb], axis=1)).astype(BF16)
    w_ab_p = jnp.concatenate(
        [_pad_lanes(jnp.concatenate([w_a[:, g * hb:(g + 1) * hb], w_bb[:, g * hb:(g + 1) * hb]], axis=1))
         for g in range(ng)], axis=1).astype(BF16)
    grp_vec = lambda x: jnp.concatenate([_pad_lanes(x[:, g * hb:(g + 1) * hb]) for g in range(ng)], axis=1)
    alog_p, dtb_p = grp_vec(a_log.astype(F32)), grp_vec(dt_bias.astype(F32))
    alog_s, dtb_s = _pad_lanes(a_log.astype(F32)), _pad_lanes(dt_bias.astype(F32))
    up1, down1 = ffn1_w_up[0].astype(BF16), ffn1_w_down[0].astype(BF16)
    up2, down2 = ffn2_w_up[0].astype(BF16), ffn2_w_down[0].astype(BF16)
    wo = w_out[0].astype(BF16)
    wpg, wpp = ple_gate[0].astype(BF16), ple_proj[0].astype(BF16)
    cw = conv_w[0].astype(F32)

    def trunk_in(x, pages=None):
        h = _rmsnorm(x, ffn1_norm)
        return _ffn(x, h, up1, down1, mix_norm, pages)

    def trunk_out(x1, o_a_, o_b_, gates, pe, pages=None):
        x2, h2 = _outproj(x1, o_a_, o_b_, gates, wo, ffn2_norm)
        x3, h3, *page_sums = _ffn(x2, h2, up2, down2, ple_norm, pages)
        return (_ple(x3, h3, pe.astype(BF16), wpg, wpp), *page_sums)

    xp = x_prompt[0]
    page_ids = page_table.reshape(-1)
    n_page_ids = page_ids.shape[0]
    pages_per_ffn = _ffn_steps(t, up1.shape[1] // 2, True)[2] * FFN_PAGES
    assert 2 * pages_per_ffn >= n_page_ids, "two prompt FFN calls must cover every cached page"
    x1, h1, sums_a = trunk_in(xp, (page_ids, cache_k, 0))
    cos_p, sin_p = _rope_tables(jnp.arange(t, dtype=jnp.int32))
    qt_p, k_p, kb_p, km_p = _rope(_matmul(h1, w_qk), q_norm, k_norm, cos_p, sin_p, True)
    v_p, vt_p = _matmul(h1, w_v, transposed_copy=True)
    raw_p = _matmul(h1, w_b)
    z_p = _matmul(h1, w_z)
    ab_p = _matmul(h1, w_ab_p)
    gates_p = _matmul(h1, w_gates)
    oa_p = _moba_prompt(qt_p, km_p, kb_p, vt_p)
    conv0 = jnp.zeros((CONV_WIDTH - 1, 3 * d), F32)
    act_p = _conv_prompt(raw_p, conv0, cw)
    ob_p, delta_p = _delta_prompt(act_p, z_p, ab_p, alog_p, dtb_p, o_norm, hb)
    y_p, sums_b = trunk_out(x1, oa_p, ob_p, gates_p, p_prompt[0, 0], (page_ids, cache_k, pages_per_ffn))
    page_sums = jnp.concatenate([sums_a, sums_b], axis=0)[:n_page_ids].reshape(bs, -1, nh, HEAD_DIM)
    conv_p = jnp.concatenate([conv0, raw_p], axis=0)[t:]

    xs = x_sample[:, 0]
    x1s, h1s = trunk_in(xs)
    pos_s = jnp.full((bs,), past_len, jnp.int32)
    cos_s, sin_s = _rope_tables(pos_s)
    q_s, k_s = _rope(_matmul(h1s, w_qk), q_norm, k_norm, cos_s, sin_s, False)
    v_s = _matmul(h1s, w_v)
    raw_s = _matmul(h1s, w_b)
    z_s = _matmul(h1s, w_z)
    ab_s = _matmul(h1s, w_ab_s)
    gates_s = _matmul(h1s, w_gates)
    by_head = lambda x: x.reshape(bs, nh, HEAD_DIM)
    sel = _block_gate(page_sums, by_head(q_s))
    sel = sel[:, :, :, 0].reshape(bs, MOBA_TOPK * nh)
    oa_s = _paged_attn(page_table, sel, by_head(q_s), by_head(k_s), by_head(v_s), cache_k, cache_v).reshape(bs, d)
    ob_s, conv_s, delta_s = _delta_step(
        raw_s.reshape(bs, 3 * nh, HEAD_DIM),
        state_conv[0].astype(F32).reshape(bs, CONV_WIDTH - 1, 3 * nh, HEAD_DIM),
        cw.reshape(CONV_WIDTH, 3 * nh, HEAD_DIM),
        z_s.reshape(bs, nh, HEAD_DIM), ab_s.reshape(bs, 1, LANE), alog_s, dtb_s, o_norm,
        state_delta[0].astype(F32))
    y_s, = trunk_out(x1s, oa_s, ob_s.reshape(bs, d), gates_s, p_sample[0, :, 0])

    heads = lambda x, n: x.reshape(1, n, -1, nh, HEAD_DIM)
    return (y_p[None], y_s[:, None],
            heads(k_p, 1), heads(v_p, 1), conv_p[None, None], delta_p[None, None],
            heads(k_s, bs), heads(v_s, bs), conv_s.reshape(1, bs, CONV_WIDTH - 1, 3 * d), delta_s[None])
```

```python
import functools
import math

import jax
import jax.numpy as jnp
from jax import lax
from jax.experimental import pallas as pl
from jax.experimental.pallas import tpu as pltpu

F32 = jnp.float32
BF16 = jnp.bfloat16
HI = lax.Precision.HIGHEST

HEAD_DIM = 128
MOBA_BLOCK = 256
MOBA_TOPK = 3
PAGE_SIZE = 128
CONV_WIDTH = 4
DELTA_CHUNK = 64
ROPE_THETA = 10000.0
RMS_EPS = 1e-6
L2_EPS = 1e-6
NEG_INF = -1e30

LANE = 128
SUBLANE = 8
VMEM_LIMIT_BYTES = 56 * 1024 * 1024


def _params(*sem):
    return pltpu.CompilerParams(dimension_semantics=sem, vmem_limit_bytes=VMEM_LIMIT_BYTES)


def _tile(n, pref, quantum):
    if n <= pref:
        return n
    t = (pref // quantum) * quantum
    while t > quantum and n % t:
        t -= quantum
    assert n % t == 0, (n, pref, quantum)
    return t


def _rms(x, gain):
    return x * lax.rsqrt(jnp.mean(x * x, axis=-1, keepdims=True) + RMS_EPS) * gain


def _dot(a, b, precision=None):
    return jnp.dot(a, b, preferred_element_type=F32, precision=precision)


def _dot_nt(a, b, precision=None):
    return lax.dot_general(a, b, (((1,), (1,)), ((), ())), preferred_element_type=F32, precision=precision)


def _dot_tn(a, b, precision=None):
    return lax.dot_general(a, b, (((0,), (0,)), ((), ())), preferred_element_type=F32, precision=precision)


def _split(x):
    hi = x.astype(BF16)
    return hi, (x - hi.astype(F32)).astype(BF16)


def _dot3_each(a_list, b_list, dot=_dot):
    a_parts = [_split(a) for a in a_list]
    b_parts = [_split(b) for b in b_list]
    passes = [(dot(a_hi, b_hi), dot(a_hi, b_lo), dot(a_lo, b_hi))
              for (a_hi, a_lo), (b_hi, b_lo) in zip(a_parts, b_parts)]
    return [hh + hl + lh for hh, hl, lh in passes]


def _rmsnorm_kernel(x_ref, g_ref, o_ref):
    o_ref[...] = _rms(x_ref[...], g_ref[...]).astype(o_ref.dtype)


def _rmsnorm(x, gain):
    m, d = x.shape
    tm = _tile(m, 512, SUBLANE)
    return pl.pallas_call(
        _rmsnorm_kernel,
        grid=(m // tm,),
        in_specs=[pl.BlockSpec((tm, d), lambda i: (i, 0)), pl.BlockSpec((1, d), lambda i: (0, 0))],
        out_specs=pl.BlockSpec((tm, d), lambda i: (i, 0)),
        out_shape=jax.ShapeDtypeStruct((m, d), BF16),
        compiler_params=_params("parallel"),
        name="rmsnorm",
    )(x, gain)


FFN_PAGES = 3
DELTA_PAGES = 16


def _page_stream(page_ids, cache, first, n_page, steps, step_of):
    _, _, page, nh, dh = cache.shape
    last = page_ids.shape[0] - 1

    def page_spec(r):
        def index(*args):
            *grid_idx, ids = args
            return (0, ids[jnp.minimum(first + step_of(*grid_idx) * n_page + r, last)], 0, 0, 0)
        return pl.BlockSpec((1, 1, page, nh, dh), index)

    out_spec = pl.BlockSpec((1, n_page, nh, dh), lambda *args: (step_of(*args[:-1]), 0, 0, 0))
    return ([page_spec(r) for r in range(n_page)], out_spec,
            jax.ShapeDtypeStruct((steps, n_page, nh, dh), F32))


def _sum_pages(page_refs, psum_ref, first=0):
    for r, page_ref in enumerate(page_refs):
        psum_ref[0, first + r] = jnp.sum(page_ref[0, 0], axis=0)


def _ffn_kernel(*refs, nf, n_page):
    refs = refs[1:] if n_page else refs
    x_ref, h_ref, wg_ref, wu_ref, wd_ref, ng_ref = refs[:6]
    page_refs = refs[6:6 + n_page]
    y_ref, hn_ref = refs[6 + n_page:8 + n_page]
    acc_ref = refs[-1]
    f = pl.program_id(1)

    @pl.when(f == 0)
    def _():
        acc_ref[...] = jnp.zeros_like(acc_ref)

    h = h_ref[...]
    g = _dot(h, wg_ref[...])
    u = _dot(h, wu_ref[...])
    a = (g * jax.nn.sigmoid(g) * u).astype(BF16)
    acc_ref[...] += _dot(a, wd_ref[...])

    if n_page:
        _sum_pages(page_refs, refs[8 + n_page])

    @pl.when(f == nf - 1)
    def _():
        y = x_ref[...] + 0.5 * acc_ref[...]
        y_ref[...] = y
        hn_ref[...] = _rms(y, ng_ref[...]).astype(hn_ref.dtype)


def _ffn_steps(m, ff, with_pages):
    tm = _tile(m, 512, SUBLANE)
    tf = _tile(ff, 256 if with_pages else 512, LANE)
    return tm, tf, (m // tm) * (ff // tf)


def _ffn(x, h, w_up, w_down, next_gain, pages=None):
    m, d = x.shape
    ff = w_down.shape[0]
    tm, tf, steps = _ffn_steps(m, ff, pages is not None)
    nf = ff // tf
    n_page = FFN_PAGES if pages is not None else 0
    in_specs = [
        pl.BlockSpec((tm, d), lambda i, f, *_: (i, 0)),
        pl.BlockSpec((tm, d), lambda i, f, *_: (i, 0)),
        pl.BlockSpec((d, tf), lambda i, f, *_: (0, f)),
        pl.BlockSpec((d, tf), lambda i, f, *_: (0, f + nf)),
        pl.BlockSpec((tf, d), lambda i, f, *_: (f, 0)),
        pl.BlockSpec((1, d), lambda i, f, *_: (0, 0)),
    ]
    out_specs = [pl.BlockSpec((tm, d), lambda i, f, *_: (i, 0)), pl.BlockSpec((tm, d), lambda i, f, *_: (i, 0))]
    out_shape = [jax.ShapeDtypeStruct((m, d), F32), jax.ShapeDtypeStruct((m, d), BF16)]
    args = [x, h, w_up, w_up, w_down, next_gain]
    prefetch = []
    if pages is not None:
        page_ids, cache, first = pages
        page_in, page_out, page_shape = _page_stream(page_ids, cache, first, n_page, steps, lambda i, f: i * nf + f)
        in_specs += page_in
        out_specs.append(page_out)
        out_shape.append(page_shape)
        args += [cache] * n_page
        prefetch = [page_ids]
    outs = pl.pallas_call(
        functools.partial(_ffn_kernel, nf=nf, n_page=n_page),
        grid_spec=pltpu.PrefetchScalarGridSpec(
            num_scalar_prefetch=len(prefetch),
            grid=(m // tm, nf),
            in_specs=in_specs,
            out_specs=out_specs,
            scratch_shapes=[pltpu.VMEM((tm, d), F32)],
        ),
        out_shape=out_shape,
        compiler_params=_params("parallel", "arbitrary"),
        name="ffn",
    )(*prefetch, *args)
    if pages is None:
        return outs
    y, hn, psum = outs
    return y, hn, psum.reshape((-1,) + psum.shape[2:])


def _mm_kernel(h_ref, w_ref, o_ref, *maybe_ot_ref):
    o = _dot(h_ref[...], w_ref[...])
    o_ref[...] = o
    for ot_ref in maybe_ot_ref:
        ot_ref[...] = o.T.astype(ot_ref.dtype)


def _matmul(h, w, transposed_copy=False):
    m, kd = h.shape
    n = w.shape[1]
    tm = _tile(m, 512, LANE if transposed_copy else SUBLANE)
    tn = _tile(n, 1024, LANE)
    out_specs = [pl.BlockSpec((tm, tn), lambda i, j: (i, j))]
    out_shape = [jax.ShapeDtypeStruct((m, n), F32)]
    if transposed_copy:
        out_specs.append(pl.BlockSpec((tn, tm), lambda i, j: (j, i)))
        out_shape.append(jax.ShapeDtypeStruct((n, m), BF16))
    outs = pl.pallas_call(
        _mm_kernel,
        grid=(m // tm, n // tn),
        in_specs=[pl.BlockSpec((tm, kd), lambda i, j: (i, 0)), pl.BlockSpec((kd, tn), lambda i, j: (0, j))],
        out_specs=out_specs,
        out_shape=out_shape,
        compiler_params=_params("parallel", "arbitrary"),
        name="proj",
    )(h, w)
    return outs if transposed_copy else outs[0]


def _outproj_kernel(x_ref, oa_ref, ob_ref, ga_ref, gb_ref, w_ref, ng_ref, y_ref, hn_ref):
    merged = jax.nn.sigmoid(ga_ref[...]) * oa_ref[...] + jax.nn.sigmoid(gb_ref[...]) * ob_ref[...]
    y = x_ref[...] + _dot(merged.astype(BF16), w_ref[...])
    y_ref[...] = y
    hn_ref[...] = _rms(y, ng_ref[...]).astype(hn_ref.dtype)


def _outproj(x, o_a, o_b, gates, w_out, next_gain):
    m, d = x.shape
    tm = _tile(m, 256, SUBLANE)
    row = lambda i: (i, 0)
    return pl.pallas_call(
        _outproj_kernel,
        grid=(m // tm,),
        in_specs=[
            pl.BlockSpec((tm, d), row),
            pl.BlockSpec((tm, d), row),
            pl.BlockSpec((tm, d), row),
            pl.BlockSpec((tm, d), lambda i: (i, 0)),
            pl.BlockSpec((tm, d), lambda i: (i, 1)),
            pl.BlockSpec((d, d), lambda i: (0, 0)),
            pl.BlockSpec((1, d), lambda i: (0, 0)),
        ],
        out_specs=[pl.BlockSpec((tm, d), row), pl.BlockSpec((tm, d), row)],
        out_shape=[jax.ShapeDtypeStruct((m, d), F32), jax.ShapeDtypeStruct((m, d), BF16)],
        compiler_params=_params("parallel"),
        name="outproj",
    )(x, o_a, o_b, gates, gates, w_out, next_gain)


def _ple_kernel(x_ref, h_ref, pe_ref, wg_ref, wp_ref, y_ref):
    gate = jax.nn.sigmoid(_dot(h_ref[...], wg_ref[...]))
    y_ref[...] = x_ref[...] + gate * _dot(pe_ref[...], wp_ref[...])


def _ple(x, h, pe, w_gate, w_proj):
    m, d = x.shape
    pd = pe.shape[1]
    tm = _tile(m, 512, SUBLANE)
    tn = _tile(d, 1024, LANE)
    return pl.pallas_call(
        _ple_kernel,
        grid=(m // tm, d // tn),
        in_specs=[
            pl.BlockSpec((tm, tn), lambda i, j: (i, j)),
            pl.BlockSpec((tm, d), lambda i, j: (i, 0)),
            pl.BlockSpec((tm, pd), lambda i, j: (i, 0)),
            pl.BlockSpec((d, tn), lambda i, j: (0, j)),
            pl.BlockSpec((pd, tn), lambda i, j: (0, j)),
        ],
        out_specs=pl.BlockSpec((tm, tn), lambda i, j: (i, j)),
        out_shape=jax.ShapeDtypeStruct((m, d), F32),
        compiler_params=_params("parallel", "arbitrary"),
        name="ple",
    )(x, h, pe, w_gate, w_proj)


def _rope_kernel(xq_ref, xk_ref, gq_ref, gk_ref, c_ref, s_ref, q_ref, k_ref, *prompt_refs, for_prompt):
    c = c_ref[...]
    s = s_ref[...]

    def norm_rot(x, gain):
        y = _rms(x, gain)
        return y * c + pltpu.roll(y, HEAD_DIM // 2, 1) * s

    q = norm_rot(xq_ref[...], gq_ref[...])
    k = norm_rot(xk_ref[...], gk_ref[...])
    k_ref[...] = k
    if for_prompt:
        kb_ref, km_ref = prompt_refs
        q_ref[...] = q.T
        kb_ref[...] = k.astype(kb_ref.dtype)
        km_ref[...] = jnp.mean(k.reshape(km_ref.shape[0], MOBA_BLOCK, HEAD_DIM), axis=1)
    else:
        q_ref[...] = q


def _rope(qk_raw, q_gain, k_gain, cos_tab, sin_tab, for_prompt):
    m, w2 = qk_raw.shape
    nh = w2 // (2 * HEAD_DIM)
    tm = _tile(m, SUBLANE * MOBA_BLOCK, SUBLANE * MOBA_BLOCK) if for_prompt else _tile(m, 1024, SUBLANE)
    blk = lambda off: pl.BlockSpec((tm, HEAD_DIM), lambda i, h: (i, h + off))
    vec = pl.BlockSpec((1, HEAD_DIM), lambda i, h: (0, 0))
    tab = pl.BlockSpec((tm, HEAD_DIM), lambda i, h: (i, 0))
    out = jax.ShapeDtypeStruct((m, nh * HEAD_DIM), F32)
    if for_prompt:
        assert m % tm == 0 and tm % (SUBLANE * MOBA_BLOCK) == 0
        out_specs = [pl.BlockSpec((HEAD_DIM, tm), lambda i, h: (h, i)), blk(0), blk(0),
                     pl.BlockSpec((tm // MOBA_BLOCK, HEAD_DIM), lambda i, h: (i, h))]
        out_shape = [jax.ShapeDtypeStruct((nh * HEAD_DIM, m), F32), out,
                     jax.ShapeDtypeStruct((m, nh * HEAD_DIM), BF16),
                     jax.ShapeDtypeStruct((m // MOBA_BLOCK, nh * HEAD_DIM), F32)]
    else:
        out_specs, out_shape = [blk(0), blk(0)], [out, out]
    return pl.pallas_call(
        functools.partial(_rope_kernel, for_prompt=for_prompt),
        grid=(m // tm, nh),
        in_specs=[blk(0), blk(nh), vec, vec, tab, tab],
        out_specs=out_specs,
        out_shape=out_shape,
        compiler_params=_params("parallel", "arbitrary"),
        name="rope",
    )(qk_raw, qk_raw, q_gain, k_gain, cos_tab, sin_tab)


def _rope_tables(pos):
    half = HEAD_DIM // 2
    inv_freq = ROPE_THETA ** (-jnp.arange(half, dtype=F32) / half)
    ang = pos.astype(F32)[:, None] * inv_freq[None, :]
    cos, sin = jnp.cos(ang), jnp.sin(ang)
    return jnp.concatenate([cos, cos], axis=1), jnp.concatenate([-sin, sin], axis=1)


def _topk_mask(gate, valid, axis):
    n = gate.shape[axis]
    idx = lax.broadcasted_iota(jnp.int32, gate.shape, axis)
    g = jnp.where(valid, gate, NEG_INF)
    sel = jnp.zeros(gate.shape, F32)
    for _ in range(MOBA_TOPK):
        top = jnp.max(g, axis=axis, keepdims=True)
        first = jnp.min(jnp.where(g == top, idx, n), axis=axis, keepdims=True)
        hit = idx == first
        sel = jnp.where(hit, jnp.where(valid, 1.0, 0.0), sel)
        g = jnp.where(hit, -jnp.inf, g)
    return sel


MOBA_HEADS = 4


def _moba_prompt_kernel(qt_ref, km_ref, kb_ref, vt_ref, o_ref, sel_ref, acc_ref, *, hp):
    i = pl.program_id(1)
    blk = MOBA_BLOCK
    q_scale = HEAD_DIM ** -0.5 * math.log2(math.e)
    heads = [slice(c * HEAD_DIM, (c + 1) * HEAD_DIM) for c in range(hp)]
    row0 = pl.multiple_of(i * blk, blk)
    k_id = lax.broadcasted_iota(jnp.int32, (blk, blk), 0)
    q_id = lax.broadcasted_iota(jnp.int32, (blk, blk), 1)

    qs, m0, l0 = [], [], []
    for c, hs in enumerate(heads):
        qt = qt_ref[hs, :]
        gate = _dot(km_ref[:, hs], qt, HI)
        bid = lax.broadcasted_iota(jnp.int32, gate.shape, 0)
        sel_ref[c] = _topk_mask(gate, bid < i, 0)
        q_c = (qt * q_scale).astype(BF16)
        s = jnp.where(k_id <= q_id, _dot(kb_ref[pl.ds(row0, blk), hs], q_c), NEG_INF)
        m_c = jnp.max(s, axis=0, keepdims=True)
        p = jnp.exp2(s - m_c)
        acc_ref[c] = _dot(vt_ref[hs, pl.ds(row0, blk)], p.astype(BF16))
        qs.append(q_c)
        m0.append(m_c)
        l0.append(jnp.sum(p, axis=0, keepdims=True))

    def body(j, carry):
        ms, ls = carry
        r0 = pl.multiple_of(j * blk, blk)
        scores = [_dot(kb_ref[pl.ds(r0, blk), hs], qs[c]) for c, hs in enumerate(heads)]
        new_m, new_l, alphas, pvs = [], [], [], []
        for c, hs in enumerate(heads):
            pick = sel_ref[c, pl.ds(j, 1), :]
            s = jnp.where(pick > 0.0, scores[c], NEG_INF)
            m_new = jnp.maximum(ms[c], jnp.max(s, axis=0, keepdims=True))
            p = jnp.exp2(s - m_new)
            alpha = jnp.exp2(ms[c] - m_new)
            new_l.append(alpha * ls[c] + jnp.sum(p, axis=0, keepdims=True))
            pvs.append(_dot(vt_ref[hs, pl.ds(r0, blk)], p.astype(BF16)))
            alphas.append(alpha)
            new_m.append(m_new)
        for c in range(hp):
            acc_ref[c] = alphas[c] * acc_ref[c] + pvs[c]
        return tuple(new_m), tuple(new_l)

    _, ls = lax.fori_loop(0, i, body, (tuple(m0), tuple(l0)))
    for c, hs in enumerate(heads):
        o_ref[:, hs] = (acc_ref[c] / ls[c]).T


def _moba_prompt(q_t, k_means, k_bf, v_t):
    w, t = q_t.shape
    nh = w // HEAD_DIM
    hp = math.gcd(nh, MOBA_HEADS)
    nb = t // MOBA_BLOCK
    assert t % MOBA_BLOCK == 0 and nb % SUBLANE == 0
    gw = hp * HEAD_DIM
    return pl.pallas_call(
        functools.partial(_moba_prompt_kernel, hp=hp),
        grid=(nh // hp, nb),
        in_specs=[pl.BlockSpec((gw, MOBA_BLOCK), lambda g, i: (g, i)),
                  pl.BlockSpec((nb, gw), lambda g, i: (0, g)),
                  pl.BlockSpec((t, gw), lambda g, i: (0, g)),
                  pl.BlockSpec((gw, t), lambda g, i: (g, 0))],
        out_specs=pl.BlockSpec((MOBA_BLOCK, gw), lambda g, i: (i, g)),
        out_shape=jax.ShapeDtypeStruct((t, w), F32),
        scratch_shapes=[pltpu.VMEM((hp, nb, MOBA_BLOCK), F32), pltpu.VMEM((hp, HEAD_DIM, MOBA_BLOCK), F32)],
        compiler_params=_params("parallel", "arbitrary"),
        name="moba_prompt",
    )(q_t, k_means, k_bf, v_t)


def _silu(x):
    return x * jax.nn.sigmoid(x)


def _softplus(x):
    return jnp.maximum(x, 0.0) + jnp.log1p(jnp.exp(-jnp.abs(x)))


def _conv_kernel(x_ref, halo_ref, prev_ref, w_ref, o_ref, *, nh):
    i = pl.program_id(0)
    j = pl.program_id(1)
    tm = x_ref.shape[0]
    base = SUBLANE - (CONV_WIDTH - 1)
    post = jnp.where(j == 0, HEAD_DIM ** -0.5, 1.0)
    for h in range(nh):
        cols = slice(h * HEAD_DIM, (h + 1) * HEAD_DIM)
        halo = jnp.where(i == 0, prev_ref[:, cols], halo_ref[:, cols])
        xp = jnp.concatenate([halo, x_ref[:, cols]], axis=0)
        w = w_ref[:, cols]
        conv = sum(xp[base + c:base + c + tm, :] * w[c:c + 1, :] for c in range(CONV_WIDTH))
        act = _silu(conv)
        inv = lax.rsqrt(jnp.sum(act * act, axis=-1, keepdims=True) + L2_EPS) * post
        o_ref[:, cols] = act * jnp.where(j == 2, 1.0, inv)


def _conv_prompt(raw, conv_prev, conv_w):
    t, cw = raw.shape
    sec = cw // 3
    nh = sec // HEAD_DIM
    tm = _tile(t, 256, SUBLANE)
    pad = SUBLANE - (CONV_WIDTH - 1)
    prev = jnp.pad(conv_prev.astype(F32), ((pad, 0), (0, 0)))
    wpad = jnp.pad(conv_w, ((0, SUBLANE - CONV_WIDTH), (0, 0)))
    rows_per = tm // SUBLANE
    return pl.pallas_call(
        functools.partial(_conv_kernel, nh=nh),
        grid=(t // tm, 3),
        in_specs=[
            pl.BlockSpec((tm, sec), lambda i, j: (i, j)),
            pl.BlockSpec((SUBLANE, sec), lambda i, j: (jnp.maximum(i * rows_per - 1, 0), j)),
            pl.BlockSpec((SUBLANE, sec), lambda i, j: (0, j)),
            pl.BlockSpec((SUBLANE, sec), lambda i, j: (0, j)),
        ],
        out_specs=pl.BlockSpec((tm, sec), lambda i, j: (i, j)),
        out_shape=jax.ShapeDtypeStruct((t, cw), F32),
        compiler_params=_params("parallel", "arbitrary"),
        name="conv_prompt",
    )(raw, raw, prev, wpad)


def _delta_prompt_kernel(ids_ref, q_ref, k_ref, v_ref, z_ref, ab_ref, alog_ref, dtb_ref, og_ref, *refs, hb, n_page):
    del ids_ref
    o_ref, s_ref, psum_ref = refs[n_page:]
    c = pl.program_id(1)
    n = DELTA_CHUNK

    @pl.when(c == 0)
    def _():
        s_ref[...] = jnp.zeros_like(s_ref)

    ab = ab_ref[...]
    g_all = -jnp.exp(alog_ref[...]) * _softplus(ab + dtb_ref[...])
    beta_all = jax.nn.sigmoid(ab)
    r_id = lax.broadcasted_iota(jnp.int32, (n, n), 0)
    c_id = lax.broadcasted_iota(jnp.int32, (n, n), 1)
    incl = r_id >= c_id
    strict = r_id > c_id
    eye = jnp.where(r_id == c_id, 1.0, 0.0)
    gcum = _dot(jnp.where(incl, 1.0, 0.0), g_all, HI)
    gcum_t = gcum.T
    og = og_ref[...]

    hs = range(hb)
    cols = [slice(h * HEAD_DIM, (h + 1) * HEAD_DIM) for h in hs]
    q = [q_ref[:, cols[h]] for h in hs]
    k = [k_ref[:, cols[h]] for h in hs]
    v = [v_ref[:, cols[h]] for h in hs]
    gc = [gcum[:, h:h + 1] for h in hs]
    bc = [beta_all[:, hb + h:hb + h + 1] for h in hs]
    eg = [jnp.exp(gc[h]) for h in hs]
    decay = [jnp.where(incl, jnp.exp(jnp.minimum(gc[h] - gcum_t[h:h + 1, :], 0.0)), 0.0) for h in hs]
    kk = _dot3_each(k, k, _dot_nt)
    low = [jnp.where(strict, kk[h] * bc[h] * decay[h], 0.0) for h in hs]
    inv = [eye - low[h] for h in hs]
    pw = low
    levels = int(math.log2(n)) - 1
    page_refs = refs[:n_page]
    for lv in range(levels):
        pw = _dot3_each(pw, pw)
        lo, hi = n_page * lv // levels, n_page * (lv + 1) // levels
        _sum_pages(page_refs[lo:hi], psum_ref, lo)
        step = _dot3_each(pw, inv)
        inv = [inv[h] + step[h] for h in hs]
    uw = _dot3_each(inv, [jnp.concatenate([v[h] * bc[h], k[h] * (bc[h] * eg[h])], axis=1) for h in hs])
    s = [s_ref[h] for h in hs]
    ws = _dot3_each([uw[h][:, HEAD_DIM:] for h in hs], s)
    v_new = [uw[h][:, :HEAD_DIM] - ws[h] for h in hs]
    g_last = [gc[h][n - 1:n, :] for h in hs]
    kv = _dot3_each([k[h] * jnp.exp(g_last[h] - gc[h]) for h in hs], v_new, _dot_tn)
    for h in hs:
        s_ref[h] = s[h] * jnp.exp(g_last[h]) + kv[h]
    qk = [_dot_nt(q[h].astype(BF16), k[h].astype(BF16)) * decay[h] for h in hs]
    o_state = [_dot((q[h] * eg[h]).astype(BF16), s[h].astype(BF16)) for h in hs]
    o_chunk = [_dot(qk[h].astype(BF16), v_new[h].astype(BF16)) for h in hs]
    for h in hs:
        o_ref[:, cols[h]] = _rms(o_state[h] + o_chunk[h], og) * _silu(z_ref[:, cols[h]])


def _delta_steps(t, nh, hb):
    return (nh // hb) * (t // DELTA_CHUNK)


def _delta_prompt(act, z, ab, alog, dtb, o_gain, hb, pages):
    t, cw = act.shape
    nh = cw // (3 * HEAD_DIM)
    ng = nh // hb
    assert t % DELTA_CHUNK == 0
    n = DELTA_CHUNK
    nc = t // n
    gw = hb * HEAD_DIM
    grp = lambda off: pl.BlockSpec((n, gw), lambda g, c, *_: (c, g + off))
    vec = pl.BlockSpec((1, LANE), lambda g, c, *_: (0, g))
    page_ids, cache, first = pages
    page_in, page_out, page_shape = _page_stream(
        page_ids, cache, first, DELTA_PAGES, _delta_steps(t, nh, hb), lambda g, c: g * nc + c)
    o, s, psum = pl.pallas_call(
        functools.partial(_delta_prompt_kernel, hb=hb, n_page=DELTA_PAGES),
        grid_spec=pltpu.PrefetchScalarGridSpec(
            num_scalar_prefetch=1,
            grid=(ng, nc),
            in_specs=[grp(0), grp(ng), grp(2 * ng), grp(0),
                      pl.BlockSpec((n, LANE), lambda g, c, *_: (c, g)), vec, vec,
                      pl.BlockSpec((1, HEAD_DIM), lambda g, c, *_: (0, 0))] + page_in,
            out_specs=[grp(0), pl.BlockSpec((hb, HEAD_DIM, HEAD_DIM), lambda g, c, *_: (g, 0, 0)), page_out],
        ),
        out_shape=[jax.ShapeDtypeStruct((t, nh * HEAD_DIM), F32),
                   jax.ShapeDtypeStruct((nh, HEAD_DIM, HEAD_DIM), F32), page_shape],
        compiler_params=_params("parallel", "arbitrary"),
        name="delta_prompt",
    )(page_ids, act, act, act, z, ab, alog, dtb, o_gain, *([cache] * DELTA_PAGES))
    return o, s, psum.reshape((-1,) + psum.shape[2:])


def _block_gate_kernel(ps_ref, q_ref, sel_ref):
    ppb = MOBA_BLOCK // PAGE_SIZE
    ps = ps_ref[0]
    nb = ps.shape[0] // ppb
    bm = jnp.sum(ps.reshape(nb, ppb, *ps.shape[1:]), axis=1) * (1.0 / MOBA_BLOCK)
    gate = jnp.sum(bm * q_ref[...], axis=-1, keepdims=True)
    gate = jnp.broadcast_to(gate, bm.shape)
    idx = lax.broadcasted_iota(jnp.int32, gate.shape, 0)
    for r in range(MOBA_TOPK):
        top = jnp.max(gate, axis=0, keepdims=True)
        first = jnp.min(jnp.where(gate == top, idx, nb), axis=0, keepdims=True)
        sel_ref[0, r] = first[0]
        gate = jnp.where(idx == first, -jnp.inf, gate)


def _block_gate(page_sums, q):
    b, n_pages, nh, dh = page_sums.shape
    ppb = MOBA_BLOCK // PAGE_SIZE
    assert dh == HEAD_DIM and n_pages % ppb == 0 and n_pages // ppb >= MOBA_TOPK
    return pl.pallas_call(
        _block_gate_kernel,
        grid=(b,),
        in_specs=[pl.BlockSpec((1, n_pages, nh, dh), lambda i: (i, 0, 0, 0)),
                  pl.BlockSpec((1, nh, dh), lambda i: (i, 0, 0))],
        out_specs=pl.BlockSpec((1, MOBA_TOPK, nh, dh), lambda i: (i, 0, 0, 0)),
        out_shape=jax.ShapeDtypeStruct((b, MOBA_TOPK, nh, dh), jnp.int32),
        compiler_params=_params("parallel"),
        name="block_gate",
    )(page_sums, q)


def _paged_attn_kernel(pt_ref, sel_ref, ck_hbm, cv_hbm, q_ref, kn_ref, vn_ref, o_ref, kbuf, vbuf, sem,
                       *, n_seq, nh):
    ppb = MOBA_BLOCK // PAGE_SIZE
    n_slab = MOBA_TOPK * ppb
    total = n_seq * nh
    scale = HEAD_DIM ** -0.5

    def slab_copies(t, slot):
        b = t // nh
        h = t % nh
        out = []
        for s in range(MOBA_TOPK):
            blk = sel_ref[b, s * nh + h]
            for r in range(ppb):
                page = pt_ref[b, blk * ppb + r]
                i = s * ppb + r
                for c, (src, dst) in enumerate(((ck_hbm, kbuf), (cv_hbm, vbuf))):
                    out.append(pltpu.make_async_copy(
                        src.at[0, page, :, pl.ds(h, 1), :], dst.at[slot, i], sem.at[c, slot, i]))
        return out

    for cp in slab_copies(0, 0):
        cp.start()

    def body(t, carry):
        slot = t % 2
        b = t // nh
        h = t % nh

        @pl.when(t + 1 < total)
        def _():
            for cp in slab_copies(t + 1, 1 - slot):
                cp.start()

        for cp in slab_copies(t, slot):
            cp.wait()

        q = q_ref[b, pl.ds(h, 1), :]
        s_own = jnp.sum(q * kn_ref[b, pl.ds(h, 1), :], axis=-1, keepdims=True) * scale
        logits = [jnp.sum(kbuf[slot, i, :, 0, :] * q, axis=-1, keepdims=True) * scale for i in range(n_slab)]
        m = s_own
        for s in logits:
            m = jnp.maximum(m, jnp.max(s, axis=0, keepdims=True))
        p_own = jnp.exp(s_own - m)
        l = p_own
        acc = p_own * vn_ref[b, pl.ds(h, 1), :]
        for i, s in enumerate(logits):
            p = jnp.exp(s - m)
            l = l + jnp.sum(p, axis=0, keepdims=True)
            acc = acc + jnp.sum(p * vbuf[slot, i, :, 0, :], axis=0, keepdims=True)
        o_ref[b, pl.ds(h, 1), :] = acc / l
        return carry

    lax.fori_loop(0, total, body, 0)


def _paged_attn(page_table, sel, q, k_new, v_new, cache_k, cache_v):
    b, nh, dh = q.shape
    page = cache_k.shape[2]
    n_slab = MOBA_TOPK * (MOBA_BLOCK // PAGE_SIZE)
    full = pl.BlockSpec((b, nh, dh), lambda i, pt, sl: (0, 0, 0))
    hbm = pl.BlockSpec(memory_space=pl.ANY)
    grid_spec = pltpu.PrefetchScalarGridSpec(
        num_scalar_prefetch=2,
        grid=(1,),
        in_specs=[hbm, hbm, full, full, full],
        out_specs=full,
        scratch_shapes=[pltpu.VMEM((2, n_slab, page, 1, dh), F32), pltpu.VMEM((2, n_slab, page, 1, dh), F32),
                        pltpu.SemaphoreType.DMA((2, 2, n_slab))],
    )
    return pl.pallas_call(
        functools.partial(_paged_attn_kernel, n_seq=b, nh=nh),
        grid_spec=grid_spec,
        out_shape=jax.ShapeDtypeStruct((b, nh, dh), F32),
        compiler_params=_params("arbitrary"),
        name="paged_attn",
    )(page_table, sel, cache_k, cache_v, q, k_new, v_new)


def _delta_step_kernel(raw_ref, conv_ref, w_ref, z_ref, ab_ref, alog_ref, dtb_ref, og_ref, s_ref,
                       o_ref, convn_ref, sn_ref, *, nh):
    raw = raw_ref[0]
    prev = conv_ref[0]
    w = w_ref[...]
    conv = raw * w[CONV_WIDTH - 1]
    for c in range(CONV_WIDTH - 1):
        conv = conv + prev[c] * w[c]
    for c in range(CONV_WIDTH - 2):
        convn_ref[0, c] = prev[c + 1]
    convn_ref[0, CONV_WIDTH - 2] = raw
    act = _silu(conv)
    inv = lax.rsqrt(jnp.sum(act * act, axis=-1, keepdims=True) + L2_EPS)
    q = act[0:nh] * inv[0:nh] * (HEAD_DIM ** -0.5)
    k = act[nh:2 * nh] * inv[nh:2 * nh]
    v = act[2 * nh:3 * nh]
    ab = ab_ref[0]
    g = -jnp.exp(alog_ref[...]) * _softplus(ab + dtb_ref[...])
    beta = jax.nn.sigmoid(ab)
    pad = jnp.zeros((HEAD_DIM - nh, HEAD_DIM), F32)
    k_t = jnp.concatenate([k, pad], axis=0).T
    q_t = jnp.concatenate([q, pad], axis=0).T
    outs = []
    for h in range(nh):
        eg = jnp.exp(g[:, h:h + 1])
        bh = beta[:, nh + h:nh + h + 1]
        s = s_ref[0, h] * eg
        kc = k_t[:, h:h + 1]
        v_new = (v[h:h + 1, :] - jnp.sum(kc * s, axis=0, keepdims=True)) * bh
        s = s + kc * v_new
        sn_ref[0, h] = s
        outs.append(jnp.sum(q_t[:, h:h + 1] * s, axis=0, keepdims=True))
    o = jnp.concatenate(outs, axis=0)
    o_ref[0] = _rms(o, og_ref[...]) * _silu(z_ref[0])


def _delta_step(raw, state_conv, conv_w, z, ab, alog, dtb, o_gain, state):
    b, rows, dh = raw.shape
    nh = rows // 3
    assert 2 * nh <= LANE
    bs = lambda *shape: pl.BlockSpec((1,) + shape, lambda i: (i,) + (0,) * len(shape))
    const = lambda *shape: pl.BlockSpec(shape, lambda i: (0,) * len(shape))
    return pl.pallas_call(
        functools.partial(_delta_step_kernel, nh=nh),
        grid=(b,),
        in_specs=[bs(rows, dh), bs(CONV_WIDTH - 1, rows, dh), const(CONV_WIDTH, rows, dh), bs(nh, dh),
                  bs(1, LANE), const(1, LANE), const(1, LANE), const(1, dh), bs(nh, dh, dh)],
        out_specs=[bs(nh, dh), bs(CONV_WIDTH - 1, rows, dh), bs(nh, dh, dh)],
        out_shape=[jax.ShapeDtypeStruct((b, nh, dh), F32),
                   jax.ShapeDtypeStruct((b, CONV_WIDTH - 1, rows, dh), F32),
                   jax.ShapeDtypeStruct((b, nh, dh, dh), F32)],
        compiler_params=_params("parallel"),
        name="delta_step",
    )(raw, state_conv, conv_w, z, ab, alog, dtb, o_gain, state)


def _pad_lanes(x):
    return jnp.pad(x, ((0, 0), (0, LANE - x.shape[1])))


def kernel(x_prompt, x_sample, cache_k, cache_v, state_conv, state_delta, page_table, p_prompt, p_sample,
           ffn1_norm, ffn1_w_up, ffn1_w_down, mix_norm, w_in, q_norm, k_norm, conv_w, a_log, dt_bias,
           o_norm, w_out, ffn2_norm, ffn2_w_up, ffn2_w_down, ple_norm, ple_gate, ple_proj):
    depth = ffn1_norm.shape[0]
    assert depth == 1, "one trunk layer"
    bp, t, d = x_prompt.shape
    bs, ts, _ = x_sample.shape
    assert bp == 1 and ts == 1
    nh = d // HEAD_DIM
    hb = math.gcd(nh, 16)
    assert 2 * hb <= LANE
    ng = nh // hb
    past_len = page_table.shape[1] * PAGE_SIZE
    assert past_len % MOBA_BLOCK == 0

    w = w_in[0]
    o_qk, o_v, o_b, o_z, o_a, o_bb, o_g = 0, 2 * d, 3 * d, 6 * d, 7 * d, 7 * d + nh, 7 * d + 2 * nh
    w_qk = w[:, o_qk:o_v].astype(BF16)
    w_v = w[:, o_v:o_b].astype(BF16)
    w_b = w[:, o_b:o_z].astype(BF16)
    w_z = w[:, o_z:o_a].astype(BF16)
    w_gates = w[:, o_g:o_g + 2 * d].astype(BF16)
    w_a, w_bb = w[:, o_a:o_bb], w[:, o_bb:o_g]
    w_ab_s = _pad_lanes(jnp.concatenate([w_a, w_bb], axis=1)).astype(BF16)
    w_ab_p = jnp.concatenate(
        [_pad_lanes(jnp.concatenate([w_a[:, g * hb:(g + 1) * hb], w_bb[:, g * hb:(g + 1) * hb]], axis=1))
         for g in range(ng)], axis=1).astype(BF16)
    grp_vec = lambda x: jnp.concatenate([_pad_lanes(x[:, g * hb:(g + 1) * hb]) for g in range(ng)], axis=1)
    alog_p, dtb_p = grp_vec(a_log.astype(F32)), grp_vec(dt_bias.astype(F32))
    alog_s, dtb_s = _pad_lanes(a_log.astype(F32)), _pad_lanes(dt_bias.astype(F32))
    up1, down1 = ffn1_w_up[0].astype(BF16), ffn1_w_down[0].astype(BF16)
    up2, down2 = ffn2_w_up[0].astype(BF16), ffn2_w_down[0].astype(BF16)
    wo = w_out[0].astype(BF16)
    wpg, wpp = ple_gate[0].astype(BF16), ple_proj[0].astype(BF16)
    cw = conv_w[0].astype(F32)

    def trunk_in(x, pages=None):
        h = _rmsnorm(x, ffn1_norm)
        return _ffn(x, h, up1, down1, mix_norm, pages)

    def trunk_out(x1, o_a_, o_b_, gates, pe, pages=None):
        x2, h2 = _outproj(x1, o_a_, o_b_, gates, wo, ffn2_norm)
        x3, h3, *page_sums = _ffn(x2, h2, up2, down2, ple_norm, pages)
        return (_ple(x3, h3, pe.astype(BF16), wpg, wpp), *page_sums)

    xp = x_prompt[0]
    page_ids = page_table.reshape(-1)
    n_page_ids = page_ids.shape[0]
    pages_per_ffn = _ffn_steps(t, up1.shape[1] // 2, True)[2] * FFN_PAGES
    pages_in_delta = _delta_steps(t, nh, hb) * DELTA_PAGES
    assert 2 * pages_per_ffn + pages_in_delta >= n_page_ids, "the prompt calls must cover every cached page"
    x1, h1, sums_a = trunk_in(xp, (page_ids, cache_k, 0))
    cos_p, sin_p = _rope_tables(jnp.arange(t, dtype=jnp.int32))
    qt_p, k_p, kb_p, km_p = _rope(_matmul(h1, w_qk), q_norm, k_norm, cos_p, sin_p, True)
    v_p, vt_p = _matmul(h1, w_v, transposed_copy=True)
    raw_p = _matmul(h1, w_b)
    z_p = _matmul(h1, w_z)
    ab_p = _matmul(h1, w_ab_p)
    gates_p = _matmul(h1, w_gates)
    oa_p = _moba_prompt(qt_p, km_p, kb_p, vt_p)
    conv0 = jnp.zeros((CONV_WIDTH - 1, 3 * d), F32)
    act_p = _conv_prompt(raw_p, conv0, cw)
    ob_p, delta_p, sums_b = _delta_prompt(act_p, z_p, ab_p, alog_p, dtb_p, o_norm, hb,
                                          (page_ids, cache_k, pages_per_ffn))
    y_p, sums_c = trunk_out(x1, oa_p, ob_p, gates_p, p_prompt[0, 0],
                            (page_ids, cache_k, pages_per_ffn + pages_in_delta))
    page_sums = jnp.concatenate([sums_a, sums_b, sums_c], axis=0)[:n_page_ids].reshape(bs, -1, nh, HEAD_DIM)
    keep = CONV_WIDTH - 1
    conv_p = raw_p[t - keep:] if t >= keep else jnp.concatenate([conv0, raw_p], axis=0)[t:]

    xs = x_sample[:, 0]
    x1s, h1s = trunk_in(xs)
    pos_s = jnp.full((bs,), past_len, jnp.int32)
    cos_s, sin_s = _rope_tables(pos_s)
    q_s, k_s = _rope(_matmul(h1s, w_qk), q_norm, k_norm, cos_s, sin_s, False)
    v_s = _matmul(h1s, w_v)
    raw_s = _matmul(h1s, w_b)
    z_s = _matmul(h1s, w_z)
    ab_s = _matmul(h1s, w_ab_s)
    gates_s = _matmul(h1s, w_gates)
    by_head = lambda x: x.reshape(bs, nh, HEAD_DIM)
    sel = _block_gate(page_sums, by_head(q_s))
    sel = sel[:, :, :, 0].reshape(bs, MOBA_TOPK * nh)
    oa_s = _paged_attn(page_table, sel, by_head(q_s), by_head(k_s), by_head(v_s), cache_k, cache_v).reshape(bs, d)
    ob_s, conv_s, delta_s = _delta_step(
        raw_s.reshape(bs, 3 * nh, HEAD_DIM),
        state_conv[0].astype(F32).reshape(bs, CONV_WIDTH - 1, 3 * nh, HEAD_DIM),
        cw.reshape(CONV_WIDTH, 3 * nh, HEAD_DIM),
        z_s.reshape(bs, nh, HEAD_DIM), ab_s.reshape(bs, 1, LANE), alog_s, dtb_s, o_norm,
        state_delta[0].astype(F32))
    y_s, = trunk_out(x1s, oa_s, ob_s.reshape(bs, d), gates_s, p_sample[0, :, 0])

    heads = lambda x, n: x.reshape(1, n, -1, nh, HEAD_DIM)
    return (y_p[None], y_s[:, None],
            heads(k_p, 1), heads(v_p, 1), conv_p[None, None], delta_p[None, None],
            heads(k_s, bs), heads(v_s, bs), conv_s.reshape(1, bs, CONV_WIDTH - 1, 3 * d), delta_s[None])
```
